```python
import jax
import jax.numpy as jnp
from jax import lax
import numpy as np


D_MODEL = 1024
BATCH = 8
SEQ = 4096
DEPTH = 4

N_MIXERS = 4
N_A = (DEPTH + 3) // 4
N_B = (DEPTH + 2) // 4
N_C = (DEPTH + 1) // 4
N_D = DEPTH // 4
HEAD_DIM = 64
NORM_EPS = 1e-5
FFN_HIDDEN = ((8 * D_MODEL + 3 * 256 - 1) // (3 * 256)) * 256

RW_HEADS = D_MODEL // HEAD_DIM
RW_N = HEAD_DIM
RW_DECAY_LORA = 64
RW_AAA_LORA = 64
RW_GATE_LORA = 128
RW_GN_EPS = 64e-5

SW_Q_HEADS = 16
SW_KV_HEADS = 2
SW_GROUP = SW_Q_HEADS // SW_KV_HEADS
SW_WINDOW = 128
SW_BLOCK = 128
ROPE_THETA = 10000.0

SG_CHUNK = 128
SG_WIDTH = 2 * D_MODEL
SG_GROUPS = 16
SG_GROUP_DIM = SG_WIDTH // SG_GROUPS

GLA_HEADS = 4
GLA_DK = D_MODEL // 2
GLA_DV = D_MODEL
GLA_HK = GLA_DK // GLA_HEADS
GLA_HV = GLA_DV // GLA_HEADS
GLA_GATE_LORA = 16
GLA_TAU = 16.0
GLA_CHUNK = 64

kernel_name = 'hybrid_interleaved_rwkv7_swa_sgu_gla'


def rmsnorm(x, g):
    xf = x.astype(jnp.float32)
    y = xf * lax.rsqrt(jnp.mean(jnp.square(xf), -1, keepdims=True) + NORM_EPS)
    return (y * g).astype(x.dtype)


def layernorm(x, g, b):
    xf = x.astype(jnp.float32)
    mu = jnp.mean(xf, -1, keepdims=True)
    var = jnp.mean(jnp.square(xf - mu), -1, keepdims=True)
    return ((xf - mu) * lax.rsqrt(var + NORM_EPS) * g + b).astype(x.dtype)


def token_shift(x):
    return jnp.pad(x[:, :-1], ((0, 0), (1, 0), (0, 0)))


def rope_tables(positions):
    f32 = jnp.float32
    inv_freq = ROPE_THETA ** (-jnp.arange(0, HEAD_DIM, 2, dtype=f32) / HEAD_DIM)
    ang = positions.astype(f32)[..., None] * inv_freq
    return jnp.cos(ang)[:, :, None, :], jnp.sin(ang)[:, :, None, :]


def apply_rope(t, cos, sin):
    half = t.shape[-1] // 2
    t1, t2 = t[..., :half], t[..., half:]
    return jnp.concatenate([t1 * cos - t2 * sin, t2 * cos + t1 * sin], -1).astype(t.dtype)


def swiglu(x, w_in, w_out):
    gate, up = jnp.split(x @ w_in, 2, axis=-1)
    return (jax.nn.silu(gate) * up) @ w_out


def rwkv7_mix(x, mu, w_rkv, w0, w1, w2, a0, a1, a2, g1, g2, k_k, k_a, r_k, gn_g, gn_b, w_o):
    B, S, D = x.shape
    H, N = RW_HEADS, RW_N
    f32 = jnp.float32
    xx = token_shift(x) - x
    xr, xw, xk, xv, xa, xg = (x + xx * mu[c] for c in range(6))
    r, k, v = jnp.einsum('cbsd,cde->cbse', jnp.stack([xr, xk, xv]), w_rkv).astype(f32)
    w = -jax.nn.softplus(-(w0 + jnp.tanh(xw @ w1) @ w2).astype(f32)) - 0.5
    a = jax.nn.sigmoid((a0 + (xa @ a1) @ a2).astype(f32))
    g = jax.nn.sigmoid(xg @ g1) @ g2
    hs = lambda t: t.reshape(B, S, H, N)
    kk = hs(k * k_k)
    kk = kk * lax.rsqrt(jnp.maximum(jnp.sum(kk * kk, -1, keepdims=True), 1e-24))
    k = k * (1.0 + (a - 1.0) * k_a)
    decay = jnp.exp(-jnp.exp(w))

    def step(state, inp):
        r_t, d_t, k_t, v_t, kk_t, a_t = inp
        sa = jnp.einsum('bhvk,bhk->bhv', state, kk_t)
        state = (state * d_t[:, :, None, :]
                 - sa[..., None] * (kk_t * a_t)[:, :, None, :]
                 + v_t[..., None] * k_t[:, :, None, :])
        return state, jnp.einsum('bhvk,bhk->bhv', state, r_t)

    seq_first = lambda t: jnp.moveaxis(t, 1, 0)
    xs = tuple(seq_first(t) for t in (hs(r), hs(decay), hs(k), hs(v), kk, hs(a)))
    _, y = lax.scan(step, jnp.zeros((B, H, N, N), f32), xs)
    y = jnp.moveaxis(y, 0, 1)
    mean = jnp.mean(y, -1, keepdims=True)
    var = jnp.mean(jnp.square(y - mean), -1, keepdims=True)
    y = ((y - mean) * lax.rsqrt(var + RW_GN_EPS)).reshape(B, S, D) * gn_g + gn_b
    bonus = (jnp.sum(hs(r) * hs(k) * r_k, -1, keepdims=True) * hs(v)).reshape(B, S, D)
    return ((y + bonus).astype(x.dtype) * g) @ w_o


def swa_sink_mix(x, cos, sin, w_qkv, b_qkv, sinks, w_o, b_o):
    B, S, _ = x.shape
    f32 = jnp.float32
    qd = SW_Q_HEADS * HEAD_DIM
    kd = SW_KV_HEADS * HEAD_DIM
    qkv = x @ w_qkv + b_qkv
    q = apply_rope(qkv[..., :qd].reshape(B, S, SW_Q_HEADS, HEAD_DIM), cos, sin)
    k = apply_rope(qkv[..., qd:qd + kd].reshape(B, S, SW_KV_HEADS, HEAD_DIM), cos, sin)
    v = qkv[..., qd + kd:].reshape(B, S, SW_KV_HEADS, HEAD_DIM)
    nb = S // SW_BLOCK
    qb = q.reshape(B, nb, SW_BLOCK, SW_KV_HEADS, SW_GROUP, HEAD_DIM)

    def band(t):
        tb = t.reshape(B, nb, SW_BLOCK, SW_KV_HEADS, HEAD_DIM)
        prev = jnp.pad(tb[:, :-1], ((0, 0), (1, 0), (0, 0), (0, 0), (0, 0)))
        return jnp.concatenate([prev, tb], axis=2)

    kw, vw = band(k), band(v)
    s = jnp.einsum('bnqhgd,bnkhd->bnhgqk', qb, kw).astype(f32) * (HEAD_DIM ** -0.5)
    qi = jnp.arange(SW_BLOCK)[:, None]
    kj = jnp.arange(2 * SW_BLOCK)[None, :]
    rel = qi + SW_BLOCK - kj
    blk = jnp.arange(nb)[:, None, None]
    valid = (rel >= 0) & (rel < SW_WINDOW) & (blk * SW_BLOCK + kj - SW_BLOCK >= 0)
    s = jnp.where(valid[None, :, None, None], s, -jnp.inf)
    sink = sinks.astype(f32).reshape(SW_KV_HEADS, SW_GROUP)[None, None, :, :, None, None]
    m = jnp.maximum(jnp.max(s, -1, keepdims=True), sink)
    p = jnp.exp(s - m)
    p = p / (jnp.sum(p, -1, keepdims=True) + jnp.exp(sink - m))
    o = jnp.einsum('bnhgqk,bnkhd->bnqhgd', p.astype(x.dtype), vw).reshape(B, S, qd)
    return o @ w_o + b_o


def sgu_chunk_mix(x, w_in, b_in, ln_g, ln_b, w_s, b_s, w_o, b_o):
    B, S, _ = x.shape
    h = jax.nn.gelu(x @ w_in + b_in, approximate=False)
    u, v = h[..., :SG_WIDTH], h[..., SG_WIDTH:]
    v = layernorm(v, ln_g, ln_b)
    nc = S // SG_CHUNK
    vb = v.reshape(B, nc, SG_CHUNK, SG_GROUPS, SG_GROUP_DIM)
    causal = jnp.tril(jnp.ones((SG_CHUNK, SG_CHUNK), dtype=bool))
    ws = jnp.where(causal[None], w_s, 0.0).astype(v.dtype)
    sv = jnp.einsum('gts,bnsgc->bntgc', ws, vb) + b_s.T[None, None, :, :, None]
    return (u * sv.reshape(B, S, SG_WIDTH)) @ w_o + b_o


def gla_mix(x, w_in, w_a2, b_a, gn_g, w_o):
    B, S, _ = x.shape
    f32 = jnp.float32
    H = GLA_HEADS
    proj = x @ w_in
    q, k, v, gate, a_low = jnp.split(
        proj, [GLA_DK, 2 * GLA_DK, 2 * GLA_DK + GLA_DV, 2 * GLA_DK + 2 * GLA_DV], axis=-1)
    log_a = jax.nn.log_sigmoid((a_low @ w_a2 + b_a).astype(f32)) / GLA_TAU
    nc = S // GLA_CHUNK
    shp_k = (B, nc, GLA_CHUNK, H, GLA_HK)
    shp_v = (B, nc, GLA_CHUNK, H, GLA_HV)
    q = q.astype(f32).reshape(shp_k) * (GLA_HK ** -0.5)
    k = k.astype(f32).reshape(shp_k)
    v = v.astype(f32).reshape(shp_v)
    bcum = jnp.cumsum(log_a.reshape(shp_k), axis=2)
    b_last = bcum[:, :, -1:]
    q_g = q * jnp.exp(bcum)
    k_g = k * jnp.exp(-bcum)
    k_s = k * jnp.exp(b_last - bcum)
    causal = jnp.tril(jnp.ones((GLA_CHUNK, GLA_CHUNK), dtype=bool))
    att = jnp.where(causal, jnp.einsum('bnihk,bnjhk->bnhij', q_g, k_g), 0.0)
    o_intra = jnp.einsum('bnhij,bnjhv->bnihv', att, v)

    def step(state, inp):
        qc, kc, vc, dc = inp
        o = jnp.einsum('bihk,bhkv->bihv', qc, state)
        state = state * dc[..., None] + jnp.einsum('bjhk,bjhv->bhkv', kc, vc)
        return state, o

    seq_first = lambda t: jnp.moveaxis(t, 1, 0)
    xs = (seq_first(q_g), seq_first(k_s), seq_first(v), seq_first(jnp.exp(b_last[:, :, 0])))
    _, o_inter = lax.scan(step, jnp.zeros((B, H, GLA_HK, GLA_HV), f32), xs)
    o = (o_intra + jnp.moveaxis(o_inter, 0, 1)).reshape(B, S, H, GLA_HV)
    o = o * lax.rsqrt(jnp.mean(jnp.square(o), -1, keepdims=True) + NORM_EPS)
    o = (o.reshape(B, S, GLA_DV) * gn_g).astype(x.dtype)
    return (o * jax.nn.silu(gate)) @ w_o


def setup_inputs(seed: int = 0) -> dict:
    key = jax.random.key(seed)
    ks = iter(jax.random.split(key, 64))
    f32 = jnp.float32
    D = D_MODEL

    def nrm(shape, scale):
        return jax.random.normal(next(ks), shape, f32) * scale

    def uni(shape, lo, hi):
        return jax.random.uniform(next(ks), shape, f32, lo, hi)

    sw_qkv_dim = (SW_Q_HEADS + 2 * SW_KV_HEADS) * HEAD_DIM
    gla_in_dim = 2 * GLA_DK + 2 * GLA_DV + GLA_GATE_LORA
    return {
        'x': nrm((BATCH, SEQ, D), 1.0),
        'positions': jnp.broadcast_to(jnp.arange(SEQ, dtype=jnp.int32), (BATCH, SEQ)),
        'norm_mix': 1.0 + nrm((DEPTH, D), 0.02),
        'norm_ffn': 1.0 + nrm((DEPTH, D), 0.02),
        'ffn_w_in': nrm((DEPTH, D, 2 * FFN_HIDDEN), D ** -0.5),
        'ffn_w_out': nrm((DEPTH, FFN_HIDDEN, D), FFN_HIDDEN ** -0.5),
        'norm_final': 1.0 + nrm((D,), 0.02),
        'rw_mu': uni((N_A, 6, D), 0.0, 1.0),
        'rw_w_rkv': nrm((N_A, 3, D, D), D ** -0.5),
        'rw_w0': nrm((N_A, D), 1.0) - 1.0,
        'rw_w1': nrm((N_A, D, RW_DECAY_LORA), D ** -0.5),
        'rw_w2': nrm((N_A, RW_DECAY_LORA, D), 0.5 * RW_DECAY_LORA ** -0.5),
        'rw_a0': nrm((N_A, D), 0.1),
        'rw_a1': nrm((N_A, D, RW_AAA_LORA), D ** -0.5),
        'rw_a2': nrm((N_A, RW_AAA_LORA, D), 0.5 * RW_AAA_LORA ** -0.5),
        'rw_g1': nrm((N_A, D, RW_GATE_LORA), D ** -0.5),
        'rw_g2': nrm((N_A, RW_GATE_LORA, D), RW_GATE_LORA ** -0.5),
        'rw_k_k': 0.85 + nrm((N_A, D), 0.02),
        'rw_k_a': 1.0 + nrm((N_A, D), 0.02),
        'rw_r_k': nrm((N_A, RW_HEADS, RW_N), 0.1),
        'rw_gn_g': 1.0 + nrm((N_A, D), 0.02),
        'rw_gn_b': nrm((N_A, D), 0.02),
        'rw_w_o': nrm((N_A, D, D), D ** -0.5),
        'sw_w_qkv': nrm((N_B, D, sw_qkv_dim), D ** -0.5),
        'sw_b_qkv': nrm((N_B, sw_qkv_dim), 0.02),
        'sw_sinks': nrm((N_B, SW_Q_HEADS), 1.0),
        'sw_w_o': nrm((N_B, SW_Q_HEADS * HEAD_DIM, D), (SW_Q_HEADS * HEAD_DIM) ** -0.5),
        'sw_b_o': nrm((N_B, D), 0.02),
        'sg_w_in': nrm((N_C, D, 2 * SG_WIDTH), D ** -0.5),
        'sg_b_in': nrm((N_C, 2 * SG_WIDTH), 0.02),
        'sg_ln_g': 1.0 + nrm((N_C, SG_WIDTH), 0.02),
        'sg_ln_b': nrm((N_C, SG_WIDTH), 0.02),
        'sg_w_s': nrm((N_C, SG_GROUPS, SG_CHUNK, SG_CHUNK), SG_CHUNK ** -0.5),
        'sg_b_s': 1.0 + nrm((N_C, SG_GROUPS, SG_CHUNK), 0.02),
        'sg_w_o': nrm((N_C, SG_WIDTH, D), SG_WIDTH ** -0.5),
        'sg_b_o': nrm((N_C, D), 0.02),
        'gla_w_in': nrm((N_D, D, gla_in_dim), D ** -0.5),
        'gla_w_a2': nrm((N_D, GLA_GATE_LORA, GLA_DK), GLA_GATE_LORA ** -0.5),
        'gla_b_a': nrm((N_D, GLA_DK), 0.1),
        'gla_gn_g': 1.0 + nrm((N_D, GLA_DV), 0.02),
        'gla_w_o': nrm((N_D, GLA_DV, D), GLA_DV ** -0.5),
    }


def reference(x, positions, norm_mix, norm_ffn, ffn_w_in, ffn_w_out, norm_final,
              rw_mu, rw_w_rkv, rw_w0, rw_w1, rw_w2, rw_a0, rw_a1, rw_a2, rw_g1, rw_g2,
              rw_k_k, rw_k_a, rw_r_k, rw_gn_g, rw_gn_b, rw_w_o,
              sw_w_qkv, sw_b_qkv, sw_sinks, sw_w_o, sw_b_o,
              sg_w_in, sg_b_in, sg_ln_g, sg_ln_b, sg_w_s, sg_b_s, sg_w_o, sg_b_o,
              gla_w_in, gla_w_a2, gla_b_a, gla_gn_g, gla_w_o):
    cos, sin = rope_tables(positions)
    h = x
    for i in range(DEPTH):
        t, j = i % N_MIXERS, i // N_MIXERS
        hn = rmsnorm(h, norm_mix[i])
        if t == 0:
            m = rwkv7_mix(hn, rw_mu[j], rw_w_rkv[j], rw_w0[j], rw_w1[j], rw_w2[j],
                          rw_a0[j], rw_a1[j], rw_a2[j], rw_g1[j], rw_g2[j],
                          rw_k_k[j], rw_k_a[j], rw_r_k[j], rw_gn_g[j], rw_gn_b[j], rw_w_o[j])
        elif t == 1:
            m = swa_sink_mix(hn, cos, sin, sw_w_qkv[j], sw_b_qkv[j], sw_sinks[j],
                             sw_w_o[j], sw_b_o[j])
        elif t == 2:
            m = sgu_chunk_mix(hn, sg_w_in[j], sg_b_in[j], sg_ln_g[j], sg_ln_b[j],
                              sg_w_s[j], sg_b_s[j], sg_w_o[j], sg_b_o[j])
        else:
            m = gla_mix(hn, gla_w_in[j], gla_w_a2[j], gla_b_a[j], gla_gn_g[j], gla_w_o[j])
        h = h + m
        h = h + swiglu(rmsnorm(h, norm_ffn[i]), ffn_w_in[i], ffn_w_out[i])
    return rmsnorm(h, norm_final)
```

```python
import functools

import jax
import jax.numpy as jnp
from jax import lax
from jax.experimental import pallas as pl
from jax.experimental.pallas import tpu as pltpu

F32 = jnp.float32
BF16 = jnp.bfloat16

HEAD_DIM = 64
NORM_EPS = 1e-5
RW_GN_EPS = 64e-5
SW_Q_HEADS = 16
SW_KV_HEADS = 2
SW_BLOCK = 128
ROPE_THETA = 10000.0
SG_CHUNK = 128
SG_GROUPS = 16
GLA_HEADS = 4
GLA_TAU = 16.0
GLA_CHUNK = 64
RW_CHUNK = 64

LANES = 128
TOKEN_TILE = 512
VMEM_LIMIT = 56 * 1024 * 1024


def _mm(a, b):
    return jnp.dot(a, b, preferred_element_type=F32)


def _mm_nt(a, b):
    return lax.dot_general(a, b, (((1,), (1,)), ((), ())), preferred_element_type=F32)


def _mm_tn(a, b):
    return lax.dot_general(a, b, (((0,), (0,)), ((), ())), preferred_element_type=F32)


def _split(x):
    hi = x.astype(BF16)
    lo = (x - hi.astype(F32)).astype(BF16)
    return hi, lo


def _mm_lhs2(a, b):
    hi, lo = _split(a)
    return _mm(hi, b) + _mm(lo, b)


def _mm_rhs2(a, b):
    hi, lo = _split(b)
    return _mm(a, hi) + _mm(a, lo)


def _mm3(a, b, dot=_mm):
    ah, al = _split(a)
    bh, bl = _split(b)
    return dot(ah, bh) + (dot(ah, bl) + dot(al, bh))


def _rms(x, g):
    return x * lax.rsqrt(jnp.mean(x * x, -1, keepdims=True) + NORM_EPS) * g


def _cparams(*sem):
    return pltpu.CompilerParams(dimension_semantics=sem, vmem_limit_bytes=VMEM_LIMIT)


def _full(shape):
    nd = len(shape)
    return pl.BlockSpec(shape, lambda *_: (0,) * nd)


def _rows(tm, n):
    return pl.BlockSpec((tm, n), lambda i: (i, 0))


def _ffn_body(*refs, hidden, fc, pre, final):
    if pre:
        h_ref, a_ref, wo_ref, bo_ref, g_ref, win_ref, wout_ref, gf_ref, o_ref = refs
        x = h_ref[...] + _mm(a_ref[...], wo_ref[...]) + bo_ref[...]
    else:
        h_ref, g_ref, win_ref, wout_ref, gf_ref, o_ref = refs
        x = h_ref[...]
    xn = _rms(x, g_ref[...]).astype(BF16)
    acc = jnp.zeros_like(x)
    for c in range(hidden // fc):
        gate = _mm(xn, win_ref[:, c * fc:(c + 1) * fc])
        up = _mm(xn, win_ref[:, hidden + c * fc:hidden + (c + 1) * fc])
        act = (gate * jax.nn.sigmoid(gate) * up).astype(BF16)
        acc = acc + _mm(act, wout_ref[c * fc:(c + 1) * fc, :])
    y = x + acc
    if final:
        y = _rms(y, gf_ref[...])
    o_ref[...] = y


def _ffn(h, g, w_in, w_out, g_final, *, pre=None, final=False):
    m, d = h.shape
    hidden = w_out.shape[0]
    tm = TOKEN_TILE
    args, specs = [h], [_rows(tm, d)]
    if pre is not None:
        a, w_o, b_o = pre
        args += [a, w_o, b_o]
        specs += [_rows(tm, a.shape[1]), _full(w_o.shape), _full(b_o.shape)]
    args += [g, w_in, w_out, g_final]
    specs += [_full(g.shape), _full(w_in.shape), _full(w_out.shape), _full(g_final.shape)]
    return pl.pallas_call(
        functools.partial(_ffn_body, hidden=hidden, fc=256, pre=pre is not None, final=final),
        out_shape=jax.ShapeDtypeStruct((m, d), F32),
        grid=(m // tm,),
        in_specs=specs,
        out_specs=_rows(tm, d),
        compiler_params=_cparams("parallel"),
        name="ffn",
    )(*args)


def _swa_qkv_body(h_ref, g_ref, w_ref, b_ref, cos_ref, sin_ref, q_ref, k_ref, v_ref, *, qd):
    xn = _rms(h_ref[...], g_ref[...]).astype(BF16)
    qkv = _mm(xn, w_ref[...]) + b_ref[...]
    cos = cos_ref[...]
    sin = sin_ref[...]
    lane = lax.broadcasted_iota(jnp.int32, cos.shape, 1)
    first_half = (lane & (HEAD_DIM - 1)) < HEAD_DIM // 2

    def rope(t):
        rot = jnp.where(first_half, pltpu.roll(t, LANES - HEAD_DIM // 2, 1), pltpu.roll(t, HEAD_DIM // 2, 1))
        return t * cos + rot * sin

    scale = HEAD_DIM ** -0.5
    for j in range(qd // LANES):
        q_ref[:, j * LANES:(j + 1) * LANES] = (rope(qkv[:, j * LANES:(j + 1) * LANES]) * scale).astype(BF16)
    kd = k_ref.shape[1]
    for j in range(kd // LANES):
        k_ref[:, j * LANES:(j + 1) * LANES] = rope(qkv[:, qd + j * LANES:qd + (j + 1) * LANES]).astype(BF16)
    v_ref[...] = qkv[:, qd + kd:].astype(BF16)


def _swa_attn_body(sink_ref, q_ref, kc_ref, kp_ref, vc_ref, vp_ref, o_ref):
    n = pl.program_id(1)
    blk = SW_BLOCK
    kcat = jnp.concatenate([kp_ref[...], kc_ref[...]], 0)
    vcat = jnp.concatenate([vp_ref[...], vc_ref[...]], 0)
    r = lax.broadcasted_iota(jnp.int32, (blk, 2 * blk), 0)
    c = lax.broadcasted_iota(jnp.int32, (blk, 2 * blk), 1)
    no_prev = jnp.where(n > 0, 0, 2 * blk)
    valid = jnp.where(c < blk, c - r - no_prev - 1, r - c + blk) >= 0
    lo = lax.broadcasted_iota(jnp.int32, (2 * blk, LANES), 1) < HEAD_DIM
    group = SW_Q_HEADS // SW_KV_HEADS
    for j in range(SW_KV_HEADS):
        kj = kcat[:, j * LANES:(j + 1) * LANES]
        vj = vcat[:, j * LANES:(j + 1) * LANES]
        zero = jnp.zeros_like(kj)
        ke = (jnp.where(lo, kj, zero), jnp.where(lo, zero, kj))
        ve = (jnp.where(lo, vj, zero), jnp.where(lo, zero, vj))
        for p in range(group // 2):
            col = (j * (group // 2) + p) * LANES
            qp = q_ref[:, col:col + LANES]
            acc = jnp.zeros((blk, LANES), F32)
            for e in range(2):
                sink = sink_ref[j * group + 2 * p + e]
                s = jnp.where(valid, _mm_nt(qp, ke[e]), -jnp.inf)
                mx = jnp.maximum(jnp.max(s, -1, keepdims=True), sink)
                pe = jnp.exp(s - mx)
                den = jnp.sum(pe, -1, keepdims=True) + jnp.exp(sink - mx)
                acc = acc + _mm(pe.astype(BF16), ve[e]) / den
            o_ref[:, col:col + LANES] = acc.astype(BF16)


def _swa_layer(h, positions, g_mix, w_qkv, b_qkv, sinks, w_o, b_o, bsz, seq):
    m, d = h.shape
    qd = SW_Q_HEADS * HEAD_DIM
    kd = SW_KV_HEADS * HEAD_DIM
    heads = [slice(base + j * HEAD_DIM, base + (j + 1) * HEAD_DIM)
             for base in (qd, qd + kd) for j in range(SW_KV_HEADS) for _ in (0, 1)]
    w = jnp.concatenate([w_qkv[:, :qd]] + [w_qkv[:, s] for s in heads], 1).astype(BF16)
    b = jnp.concatenate([b_qkv[:qd]] + [b_qkv[s] for s in heads])[None]
    inv_freq = ROPE_THETA ** (-jnp.arange(0, HEAD_DIM, 2, dtype=F32) / HEAD_DIM)
    ang = positions.astype(F32).reshape(m, 1) * inv_freq
    cos = jnp.tile(jnp.cos(ang), (1, 4))
    sin = jnp.tile(jnp.concatenate([-jnp.sin(ang), jnp.sin(ang)], -1), (1, 2))
    tm = TOKEN_TILE
    kdd = 2 * kd
    q, k, v = pl.pallas_call(
        functools.partial(_swa_qkv_body, qd=qd),
        out_shape=(jax.ShapeDtypeStruct((m, qd), BF16), jax.ShapeDtypeStruct((m, kdd), BF16),
                   jax.ShapeDtypeStruct((m, kdd), BF16)),
        grid=(m // tm,),
        in_specs=[_rows(tm, d), _full(g_mix.shape), _full(w.shape), _full(b.shape), _rows(tm, LANES), _rows(tm, LANES)],
        out_specs=(_rows(tm, qd), _rows(tm, kdd), _rows(tm, kdd)),
        compiler_params=_cparams("parallel"),
        name="swa_qkv",
    )(h, g_mix, w, b, cos, sin)
    nb = seq // SW_BLOCK
    cur = lambda n_: pl.BlockSpec((None, SW_BLOCK, n_), lambda b_, i: (b_, i, 0))
    prev = lambda n_: pl.BlockSpec((None, SW_BLOCK, n_), lambda b_, i: (b_, jnp.maximum(i - 1, 0), 0))
    k3 = k.reshape(bsz, seq, kdd)
    v3 = v.reshape(bsz, seq, kdd)
    o = pl.pallas_call(
        _swa_attn_body,
        out_shape=jax.ShapeDtypeStruct((bsz, seq, qd), BF16),
        grid=(bsz, nb),
        in_specs=[pl.BlockSpec(memory_space=pltpu.SMEM), cur(qd), cur(kdd), prev(kdd), cur(kdd), prev(kdd)],
        out_specs=cur(qd),
        compiler_params=_cparams("parallel", "parallel"),
        name="swa_attn",
    )(sinks, q.reshape(bsz, seq, qd), k3, k3, v3, v3)
    return o.reshape(m, qd), w_o.astype(BF16), b_o[None]


def _sgu_body(h_ref, g_ref, win_ref, bin_ref, lng_ref, lnb_ref, ws_ref, bs_ref, wo_ref, bo_ref, o_ref, z_ref, *, width):
    x = h_ref[...]
    tm = x.shape[0]
    xn = _rms(x, g_ref[...]).astype(BF16)
    gelu = lambda t: 0.5 * t * (1.0 + lax.erf(t * (0.5 ** 0.5)))
    v = gelu(_mm(xn, win_ref[:, width:]) + bin_ref[:, width:])
    mu = jnp.mean(v, -1, keepdims=True)
    vc = v - mu
    var = jnp.mean(vc * vc, -1, keepdims=True)
    vn = (vc * lax.rsqrt(var + NORM_EPS) * lng_ref[...] + lnb_ref[...]).astype(BF16)
    gdim = width // SG_GROUPS
    for gi in range(SG_GROUPS):
        cs = slice(gi * gdim, (gi + 1) * gdim)
        u = gelu(_mm(xn, win_ref[:, cs]) + bin_ref[:, cs])
        for q in range(tm // SG_CHUNK):
            rs = slice(q * SG_CHUNK, (q + 1) * SG_CHUNK)
            sv = _mm(ws_ref[gi], vn[rs, cs]) + bs_ref[gi]
            z_ref[rs, cs] = (u[rs] * sv).astype(BF16)
    o_ref[...] = x + _mm(z_ref[...], wo_ref[...]) + bo_ref[...]


def _sgu_layer(h, g_mix, w_in, b_in, ln_g, ln_b, w_s, b_s, w_o, b_o):
    m, d = h.shape
    width = w_o.shape[0]
    causal = jnp.tril(jnp.ones((SG_CHUNK, SG_CHUNK), dtype=bool))
    ws = jnp.where(causal[None], w_s, 0.0).astype(BF16)
    bs = jnp.broadcast_to(b_s[:, :, None], (SG_GROUPS, SG_CHUNK, width // SG_GROUPS))
    args = (h, g_mix, w_in.astype(BF16), b_in[None], ln_g[None], ln_b[None], ws, bs, w_o.astype(BF16), b_o[None])
    tm = TOKEN_TILE
    return pl.pallas_call(
        functools.partial(_sgu_body, width=width),
        out_shape=jax.ShapeDtypeStruct((m, d), F32),
        grid=(m // tm,),
        in_specs=[_rows(tm, d)] + [_full(a.shape) for a in args[1:]],
        out_specs=_rows(tm, d),
        scratch_shapes=[pltpu.VMEM((tm, width), BF16)],
        compiler_params=_cparams("parallel"),
        name="sgu",
    )(*args)


def _gla_body(h_ref, g_ref, win_ref, wa1_ref, wa2_ref, ba_ref, gng_ref, wo_ref, o_ref, st_ref, oc_ref, *, dk, dv):
    @pl.when(pl.program_id(1) == 0)
    def _():
        st_ref[...] = jnp.zeros_like(st_ref)

    x = h_ref[...]
    tm = x.shape[0]
    xn = _rms(x, g_ref[...]).astype(BF16)
    hk, hv = dk // GLA_HEADS, dv // GLA_HEADS
    q_all = _mm(xn, win_ref[:, :dk]) * (hk ** -0.5)
    k_all = _mm(xn, win_ref[:, dk:2 * dk])
    v_all = _mm(xn, win_ref[:, 2 * dk:2 * dk + dv])
    a_low = _mm(xn, wa1_ref[...])
    log_a = jax.nn.log_sigmoid(_mm(a_low.astype(BF16), wa2_ref[...]) + ba_ref[...]) / GLA_TAU
    cl = GLA_CHUNK
    ri = lax.broadcasted_iota(jnp.int32, (cl, cl), 0)
    ci = lax.broadcasted_iota(jnp.int32, (cl, cl), 1)
    causal = ci <= ri
    tri = causal.astype(BF16)
    for c in range(tm // cl):
        rs = slice(c * cl, (c + 1) * cl)
        bcum = _mm_rhs2(tri, log_a[rs])
        b_last = bcum[cl - 1:cl]
        eb = jnp.exp(bcum)
        q_g = q_all[rs] * eb
        k_g = k_all[rs] * jnp.exp(-bcum)
        k_s = k_all[rs] * jnp.exp(b_last - bcum)
        d_last = jnp.exp(b_last)
        for hh in range(GLA_HEADS):
            ks_ = slice(hh * hk, (hh + 1) * hk)
            vs_ = slice(hh * hv, (hh + 1) * hv)
            qh = q_g[:, ks_].astype(BF16)
            vh = v_all[rs, vs_].astype(BF16)
            att = jnp.where(causal, _mm_nt(qh, k_g[:, ks_].astype(BF16)), 0.0)
            st = st_ref[hh]
            oc_ref[rs, vs_] = _mm(att.astype(BF16), vh) + _mm_nt(qh, st.astype(BF16))
            st_ref[hh] = st * d_last[:, ks_] + _mm_tn(vh, k_s[:, ks_].astype(BF16))
    gate = _mm(xn, win_ref[:, 2 * dk + dv:])
    gate = gate * jax.nn.sigmoid(gate)
    for hh in range(GLA_HEADS):
        vs_ = slice(hh * hv, (hh + 1) * hv)
        o = oc_ref[:, vs_]
        o = o * lax.rsqrt(jnp.mean(o * o, -1, keepdims=True) + NORM_EPS) * gng_ref[:, vs_]
        oc_ref[:, vs_] = o * gate[:, vs_]
    o_ref[...] = x + _mm(oc_ref[...].astype(BF16), wo_ref[...])


def _gla_layer(h, g_mix, w_in, w_a2, b_a, gn_g, w_o, bsz, seq):
    m, d = h.shape
    dk = w_a2.shape[1]
    dv = w_o.shape[0]
    lora = w_a2.shape[0]
    main = 2 * dk + 2 * dv
    wa1 = jnp.pad(w_in[:, main:], ((0, 0), (0, LANES - lora))).astype(BF16)
    wa2 = jnp.pad(w_a2, ((0, LANES - lora), (0, 0))).astype(BF16)
    args = (h.reshape(bsz, seq, d), g_mix, w_in[:, :main].astype(BF16), wa1, wa2, b_a[None], gn_g[None], w_o.astype(BF16))
    tm = TOKEN_TILE
    tile = pl.BlockSpec((None, tm, d), lambda b_, i: (b_, i, 0))
    out = pl.pallas_call(
        functools.partial(_gla_body, dk=dk, dv=dv),
        out_shape=jax.ShapeDtypeStruct((bsz, seq, d), F32),
        grid=(bsz, seq // tm),
        in_specs=[tile] + [_full(a.shape) for a in args[1:]],
        out_specs=tile,
        scratch_shapes=[pltpu.VMEM((GLA_HEADS, dv // GLA_HEADS, dk // GLA_HEADS), F32), pltpu.VMEM((tm, dv), F32)],
        compiler_params=_cparams("parallel", "arbitrary"),
        name="gla",
    )(*args)
    return out.reshape(m, d)


def _seg_sum(x, seg):
    return _mm_lhs2(x, seg)


def _seg_bcast(s, seg_t):
    return _mm_lhs2(s, seg_t)


def _rw_proj_body(h_ref, hs_ref, g_ref, mu_ref, wrkv_ref, w0_ref, w1_ref, w2_ref, a0_ref, a1_ref, a2_ref,
                  g1_ref, g2_ref, kk_ref, ka_ref, seg_ref, segt_ref,
                  r_out, ld_out, k_out, v_out, kkn_out, b_out, g_out):
    g = g_ref[...]
    xn = _rms(h_ref[...], g)
    xx = _rms(hs_ref[...], g) - xn
    mix = lambda c: (xn + xx * mu_ref[c:c + 1]).astype(BF16)
    r = _mm(mix(0), wrkv_ref[0])
    k = _mm(mix(2), wrkv_ref[1])
    v = _mm(mix(3), wrkv_ref[2])
    w_pre = w0_ref[...] + _mm(jnp.tanh(_mm(mix(1), w1_ref[...])).astype(BF16), w2_ref[...])
    w = -jax.nn.softplus(-w_pre) - 0.5
    a = jax.nn.sigmoid(a0_ref[...] + _mm(_mm(mix(4), a1_ref[...]).astype(BF16), a2_ref[...]))
    gate = _mm(jax.nn.sigmoid(_mm(mix(5), g1_ref[...])).astype(BF16), g2_ref[...])
    kk = k * kk_ref[...]
    ss = _seg_sum(kk * kk, seg_ref[...])
    kk = kk * _seg_bcast(lax.rsqrt(jnp.maximum(ss, 1e-24)), segt_ref[...])
    r_out[...] = r
    ld_out[...] = -jnp.exp(w)
    k_out[...] = k * (1.0 + (a - 1.0) * ka_ref[...])
    v_out[...] = v
    kkn_out[...] = kk
    b_out[...] = kk * a
    g_out[...] = gate


def _stack_heads(z):
    lo = lax.broadcasted_iota(jnp.int32, z.shape, 1) < HEAD_DIM
    zero = jnp.zeros_like(z)
    return jnp.concatenate([jnp.where(lo, z, zero), jnp.where(lo, zero, z)], 0)


def _rw_scan_body(r_ref, ld_ref, k_ref, v_ref, kk_ref, b_ref, y_ref, st_ref):
    @pl.when(pl.program_id(1) == 0)
    def _():
        st_ref[...] = jnp.zeros_like(st_ref)

    cl = r_ref.shape[0]
    d = r_ref.shape[1]
    ri = lax.broadcasted_iota(jnp.int32, (cl, cl), 0)
    ci = lax.broadcasted_iota(jnp.int32, (cl, cl), 1)
    tri = (ci <= ri).astype(BF16)
    ld = ld_ref[...]
    hi, lo = _split(ld)
    lo2 = (ld - hi.astype(F32) - lo.astype(F32)).astype(BF16)
    cum = _mm(tri, hi) + (_mm(tri, lo) + _mm(tri, lo2))
    p_in = jnp.exp(cum)
    p_ex = jnp.exp(cum - ld)
    p_inv = jnp.exp(-cum)
    to_end = jnp.exp(cum[cl - 1:cl] - cum)
    p_end = jnp.exp(cum[cl - 1:cl])
    r_t = r_ref[...] * p_in
    kk_t = kk_ref[...] * p_ex
    b_t = b_ref[...] * p_inv
    k_t = k_ref[...] * p_inv
    b_e = b_ref[...] * to_end
    k_e = k_ref[...] * to_end
    v = v_ref[...]

    c2 = 2 * cl
    r2 = lax.broadcasted_iota(jnp.int32, (c2, c2), 0)
    q2 = lax.broadcasted_iota(jnp.int32, (c2, c2), 1)
    same = (r2 < cl) == (q2 < cl)
    strict = same & (q2 < r2)
    incl = same & (q2 <= r2)
    eye = (r2 == q2).astype(F32)

    for p in range(d // LANES):
        cs = slice(p * LANES, (p + 1) * LANES)
        kks = _stack_heads(kk_t[:, cs])
        rs_ = _stack_heads(r_t[:, cs])
        bs = _stack_heads(b_t[:, cs])
        ks = _stack_heads(k_t[:, cs])
        vs = _stack_heads(v[:, cs])
        a_ab = jnp.where(strict, _mm3(kks, bs, _mm_nt), 0.0)
        a_ak = jnp.where(strict, _mm3(kks, ks, _mm_nt), 0.0)
        a_rb = jnp.where(incl, _mm3(rs_, bs, _mm_nt), 0.0)
        a_rk = jnp.where(incl, _mm3(rs_, ks, _mm_nt), 0.0)
        pw = -a_ab
        t_inv = eye + pw
        steps = cl.bit_length() - 1
        for _ in range(steps - 1):
            pw = _mm3(pw, pw)
            t_inv = t_inv + _mm3(t_inv, pw)
        g_mat = _mm3(t_inv, kks)
        u_mat = _mm3(t_inv, _mm3(a_ak, vs))
        st = st_ref[p]
        ws = -(_mm3(g_mat, st, _mm_nt) + u_mat)
        ys = _mm3(rs_, st, _mm_nt) + _mm3(a_rb, ws) + _mm3(a_rk, vs)
        y_ref[:, cs] = ys[:cl] + ys[cl:]
        bes = _stack_heads(b_e[:, cs])
        kes = _stack_heads(k_e[:, cs])
        st_ref[p] = st * p_end[:, cs] + _mm3(ws, bes, _mm_tn) + _mm3(vs, kes, _mm_tn)


def _rw_out_body(h_ref, y_ref, r_ref, k_ref, v_ref, g_ref, rk_ref, gng_ref, gnb_ref, seg_ref, segt_ref, wo_ref, o_ref):
    seg = seg_ref[...]
    segt = segt_ref[...]
    y = y_ref[...]
    inv_n = 1.0 / HEAD_DIM
    mean = _seg_bcast(_seg_sum(y, seg) * inv_n, segt)
    yc = y - mean
    var = _seg_sum(yc * yc, seg) * inv_n
    yn = yc * _seg_bcast(lax.rsqrt(var + RW_GN_EPS), segt) * gng_ref[...] + gnb_ref[...]
    bonus = _seg_bcast(_seg_sum(r_ref[...] * k_ref[...] * rk_ref[...], seg), segt) * v_ref[...]
    o_ref[...] = h_ref[...] + _mm(((yn + bonus) * g_ref[...]).astype(BF16), wo_ref[...])


def _rwkv_layer(h, g_mix, mu, w_rkv, w0, w1, w2, a0, a1, a2, g1, g2, k_k, k_a, r_k, gn_g, gn_b, w_o, bsz, seq):
    m, d = h.shape
    heads = d // HEAD_DIM
    hs = jnp.pad(h.reshape(bsz, seq, d)[:, :-1], ((0, 0), (1, 0), (0, 0))).reshape(m, d)
    seg = (jnp.arange(d)[:, None] // HEAD_DIM == jnp.arange(LANES)[None, :]).astype(BF16)
    segt = seg.T
    row = lambda t: t.reshape(1, -1)
    tm = TOKEN_TILE
    proj_args = (h, hs, g_mix, mu, w_rkv.astype(BF16), row(w0), w1.astype(BF16), w2.astype(BF16), row(a0),
                 a1.astype(BF16), a2.astype(BF16), g1.astype(BF16), g2.astype(BF16), row(k_k), row(k_a), seg, segt)
    outs = pl.pallas_call(
        _rw_proj_body,
        out_shape=tuple(jax.ShapeDtypeStruct((m, d), F32) for _ in range(7)),
        grid=(m // tm,),
        in_specs=[_rows(tm, d), _rows(tm, d)] + [_full(a.shape) for a in proj_args[2:]],
        out_specs=tuple(_rows(tm, d) for _ in range(7)),
        compiler_params=_cparams("parallel"),
        name="rwkv_proj",
    )(*proj_args)
    r, ld, k, v, kk, b, gate = outs
    cl = RW_CHUNK
    blk = pl.BlockSpec((None, cl, d), lambda b_, i: (b_, i, 0))
    to3 = lambda t: t.reshape(bsz, seq, d)
    y = pl.pallas_call(
        _rw_scan_body,
        out_shape=jax.ShapeDtypeStruct((bsz, seq, d), F32),
        grid=(bsz, seq // cl),
        in_specs=[blk] * 6,
        out_specs=blk,
        scratch_shapes=[pltpu.VMEM((d // LANES, LANES, LANES), F32)],
        compiler_params=_cparams("parallel", "arbitrary"),
        name="rwkv_scan",
    )(to3(r), to3(ld), to3(k), to3(v), to3(kk), to3(b))
    out_args = (h, y.reshape(m, d), r, k, v, gate, row(r_k), row(gn_g), row(gn_b), seg, segt, w_o.astype(BF16))
    return pl.pallas_call(
        _rw_out_body,
        out_shape=jax.ShapeDtypeStruct((m, d), F32),
        grid=(m // tm,),
        in_specs=[_rows(tm, d)] * 6 + [_full(a.shape) for a in out_args[6:]],
        out_specs=_rows(tm, d),
        compiler_params=_cparams("parallel"),
        name="rwkv_out",
    )(*out_args)


def kernel(x, positions, norm_mix, norm_ffn, ffn_w_in, ffn_w_out, norm_final, rw_mu, rw_w_rkv, rw_w0, rw_w1, rw_w2, rw_a0, rw_a1, rw_a2, rw_g1, rw_g2, rw_k_k, rw_k_a, rw_r_k, rw_gn_g, rw_gn_b, rw_w_o, sw_w_qkv, sw_b_qkv, sw_sinks, sw_w_o, sw_b_o, sg_w_in, sg_b_in, sg_ln_g, sg_ln_b, sg_w_s, sg_b_s, sg_w_o, sg_b_o, gla_w_in, gla_w_a2, gla_b_a, gla_gn_g, gla_w_o):
    bsz, seq, d = x.shape
    depth = norm_mix.shape[0]
    h = x.reshape(bsz * seq, d)
    g_final = norm_final[None]
    for i in range(depth):
        t, j = i % 4, i // 4
        g_mix = norm_mix[i][None]
        pre = None
        if t == 0:
            h = _rwkv_layer(h, g_mix, rw_mu[j], rw_w_rkv[j], rw_w0[j], rw_w1[j], rw_w2[j], rw_a0[j], rw_a1[j],
                            rw_a2[j], rw_g1[j], rw_g2[j], rw_k_k[j], rw_k_a[j], rw_r_k[j], rw_gn_g[j], rw_gn_b[j],
                            rw_w_o[j], bsz, seq)
        elif t == 1:
            pre = _swa_layer(h, positions, g_mix, sw_w_qkv[j], sw_b_qkv[j], sw_sinks[j], sw_w_o[j], sw_b_o[j], bsz, seq)
        elif t == 2:
            h = _sgu_layer(h, g_mix, sg_w_in[j], sg_b_in[j], sg_ln_g[j], sg_ln_b[j], sg_w_s[j], sg_b_s[j],
                           sg_w_o[j], sg_b_o[j])
        else:
            h = _gla_layer(h, g_mix, gla_w_in[j], gla_w_a2[j], gla_b_a[j], gla_gn_g[j], gla_w_o[j], bsz, seq)
        h = _ffn(h, norm_ffn[i][None], ffn_w_in[i].astype(BF16), ffn_w_out[i].astype(BF16), g_final,
                 pre=pre, final=i == depth - 1)
    return h.reshape(bsz, seq, d)
```

```python
import functools

import jax
import jax.numpy as jnp
from jax import lax
from jax.experimental import pallas as pl
from jax.experimental.pallas import tpu as pltpu

F32 = jnp.float32
BF16 = jnp.bfloat16

HEAD_DIM = 64
NORM_EPS = 1e-5
RW_GN_EPS = 64e-5
SW_Q_HEADS = 16
SW_KV_HEADS = 2
SW_BLOCK = 128
ROPE_THETA = 10000.0
SG_CHUNK = 128
SG_GROUPS = 16
GLA_HEADS = 4
GLA_TAU = 16.0
GLA_CHUNK = 64
RW_CHUNK = 64

LANES = 128
SUBLANES = 8
TOKEN_TILE = 512
VMEM_LIMIT = 56 * 1024 * 1024


def _mm(a, b):
    return jnp.dot(a, b, preferred_element_type=F32)


def _mm_nt(a, b):
    return lax.dot_general(a, b, (((1,), (1,)), ((), ())), preferred_element_type=F32)


def _mm_tn(a, b):
    return lax.dot_general(a, b, (((0,), (0,)), ((), ())), preferred_element_type=F32)


def _split(x):
    hi = x.astype(BF16)
    lo = (x - hi.astype(F32)).astype(BF16)
    return hi, lo


def _mm_lhs2(a, b):
    hi, lo = _split(a)
    return _mm(hi, b) + _mm(lo, b)


def _mm_rhs2(a, b):
    hi, lo = _split(b)
    return _mm(a, hi) + _mm(a, lo)


def _rms(x, g):
    return x * lax.rsqrt(jnp.mean(x * x, -1, keepdims=True) + NORM_EPS) * g


def _cparams(*sem):
    return pltpu.CompilerParams(dimension_semantics=sem, vmem_limit_bytes=VMEM_LIMIT)


def _full(shape):
    nd = len(shape)
    return pl.BlockSpec(shape, lambda *_: (0,) * nd)


def _rows(tm, n):
    return pl.BlockSpec((tm, n), lambda i: (i, 0))


def _ffn_body(*refs, hidden, fc, pre, final):
    if pre:
        h_ref, a_ref, wo_ref, bo_ref, g_ref, win_ref, wout_ref, gf_ref, o_ref = refs
        x = h_ref[...] + _mm(a_ref[...], wo_ref[...]) + bo_ref[...]
    else:
        h_ref, g_ref, win_ref, wout_ref, gf_ref, o_ref = refs
        x = h_ref[...]
    xn = _rms(x, g_ref[...]).astype(BF16)
    acc = jnp.zeros_like(x)
    for c in range(hidden // fc):
        gate = _mm(xn, win_ref[:, c * fc:(c + 1) * fc])
        up = _mm(xn, win_ref[:, hidden + c * fc:hidden + (c + 1) * fc])
        act = (gate * jax.nn.sigmoid(gate) * up).astype(BF16)
        acc = acc + _mm(act, wout_ref[c * fc:(c + 1) * fc, :])
    y = x + acc
    if final:
        y = _rms(y, gf_ref[...])
    o_ref[...] = y


def _ffn(h, g, w_in, w_out, g_final, *, pre=None, final=False):
    m, d = h.shape
    hidden = w_out.shape[0]
    tm = TOKEN_TILE
    args, specs = [h], [_rows(tm, d)]
    if pre is not None:
        a, w_o, b_o = pre
        args += [a, w_o, b_o]
        specs += [_rows(tm, a.shape[1]), _full(w_o.shape), _full(b_o.shape)]
    args += [g, w_in, w_out, g_final]
    specs += [_full(g.shape), _full(w_in.shape), _full(w_out.shape), _full(g_final.shape)]
    return pl.pallas_call(
        functools.partial(_ffn_body, hidden=hidden, fc=256, pre=pre is not None, final=final),
        out_shape=jax.ShapeDtypeStruct((m, d), F32),
        grid=(m // tm,),
        in_specs=specs,
        out_specs=_rows(tm, d),
        compiler_params=_cparams("parallel"),
        name="ffn",
    )(*args)


def _swa_qkv_body(h_ref, g_ref, w_ref, b_ref, cos_ref, sin_ref, q_ref, k_ref, v_ref, *, qd):
    xn = _rms(h_ref[...], g_ref[...]).astype(BF16)
    qkv = _mm(xn, w_ref[...]) + b_ref[...]
    cos = cos_ref[...]
    sin = sin_ref[...]
    lane = lax.broadcasted_iota(jnp.int32, cos.shape, 1)
    first_half = (lane & (HEAD_DIM - 1)) < HEAD_DIM // 2

    def rope(t):
        rot = jnp.where(first_half, pltpu.roll(t, LANES - HEAD_DIM // 2, 1), pltpu.roll(t, HEAD_DIM // 2, 1))
        return t * cos + rot * sin

    scale = HEAD_DIM ** -0.5
    for j in range(qd // LANES):
        q_ref[:, j * LANES:(j + 1) * LANES] = (rope(qkv[:, j * LANES:(j + 1) * LANES]) * scale).astype(BF16)
    kd = k_ref.shape[1]
    for j in range(kd // LANES):
        k_ref[:, j * LANES:(j + 1) * LANES] = rope(qkv[:, qd + j * LANES:qd + (j + 1) * LANES]).astype(BF16)
    v_ref[...] = qkv[:, qd + kd:].astype(BF16)


def _swa_attn_body(sink_ref, q_ref, kc_ref, kp_ref, vc_ref, vp_ref, o_ref):
    n = pl.program_id(1)
    blk = SW_BLOCK
    kcat = jnp.concatenate([kp_ref[...], kc_ref[...]], 0)
    vcat = jnp.concatenate([vp_ref[...], vc_ref[...]], 0)
    r = lax.broadcasted_iota(jnp.int32, (blk, 2 * blk), 0)
    c = lax.broadcasted_iota(jnp.int32, (blk, 2 * blk), 1)
    no_prev = jnp.where(n > 0, 0, 2 * blk)
    valid = jnp.where(c < blk, c - r - no_prev - 1, r - c + blk) >= 0
    lo = lax.broadcasted_iota(jnp.int32, (2 * blk, LANES), 1) < HEAD_DIM
    group = SW_Q_HEADS // SW_KV_HEADS
    for j in range(SW_KV_HEADS):
        kj = kcat[:, j * LANES:(j + 1) * LANES]
        vj = vcat[:, j * LANES:(j + 1) * LANES]
        zero = jnp.zeros_like(kj)
        ke = (jnp.where(lo, kj, zero), jnp.where(lo, zero, kj))
        ve = (jnp.where(lo, vj, zero), jnp.where(lo, zero, vj))
        for p in range(group // 2):
            col = (j * (group // 2) + p) * LANES
            qp = q_ref[:, col:col + LANES]
            acc = jnp.zeros((blk, LANES), F32)
            for e in range(2):
                sink = sink_ref[j * group + 2 * p + e]
                s = jnp.where(valid, _mm_nt(qp, ke[e]), -jnp.inf)
                mx = jnp.maximum(jnp.max(s, -1, keepdims=True), sink)
                pe = jnp.exp(s - mx)
                den = jnp.sum(pe, -1, keepdims=True) + jnp.exp(sink - mx)
                acc = acc + _mm(pe.astype(BF16), ve[e]) / den
            o_ref[:, col:col + LANES] = acc.astype(BF16)


def _swa_layer(h, positions, g_mix, w_qkv, b_qkv, sinks, w_o, b_o, bsz, seq):
    m, d = h.shape
    qd = SW_Q_HEADS * HEAD_DIM
    kd = SW_KV_HEADS * HEAD_DIM
    heads = [slice(base + j * HEAD_DIM, base + (j + 1) * HEAD_DIM)
             for base in (qd, qd + kd) for j in range(SW_KV_HEADS) for _ in (0, 1)]
    w = jnp.concatenate([w_qkv[:, :qd]] + [w_qkv[:, s] for s in heads], 1).astype(BF16)
    b = jnp.concatenate([b_qkv[:qd]] + [b_qkv[s] for s in heads])[None]
    inv_freq = ROPE_THETA ** (-jnp.arange(0, HEAD_DIM, 2, dtype=F32) / HEAD_DIM)
    ang = positions.astype(F32).reshape(m, 1) * inv_freq
    cos = jnp.tile(jnp.cos(ang), (1, 4))
    sin = jnp.tile(jnp.concatenate([-jnp.sin(ang), jnp.sin(ang)], -1), (1, 2))
    tm = TOKEN_TILE
    kdd = 2 * kd
    q, k, v = pl.pallas_call(
        functools.partial(_swa_qkv_body, qd=qd),
        out_shape=(jax.ShapeDtypeStruct((m, qd), BF16), jax.ShapeDtypeStruct((m, kdd), BF16),
                   jax.ShapeDtypeStruct((m, kdd), BF16)),
        grid=(m // tm,),
        in_specs=[_rows(tm, d), _full(g_mix.shape), _full(w.shape), _full(b.shape), _rows(tm, LANES), _rows(tm, LANES)],
        out_specs=(_rows(tm, qd), _rows(tm, kdd), _rows(tm, kdd)),
        compiler_params=_cparams("parallel"),
        name="swa_qkv",
    )(h, g_mix, w, b, cos, sin)
    nb = seq // SW_BLOCK
    cur = lambda n_: pl.BlockSpec((None, SW_BLOCK, n_), lambda b_, i: (b_, i, 0))
    prev = lambda n_: pl.BlockSpec((None, SW_BLOCK, n_), lambda b_, i: (b_, jnp.maximum(i - 1, 0), 0))
    k3 = k.reshape(bsz, seq, kdd)
    v3 = v.reshape(bsz, seq, kdd)
    o = pl.pallas_call(
        _swa_attn_body,
        out_shape=jax.ShapeDtypeStruct((bsz, seq, qd), BF16),
        grid=(bsz, nb),
        in_specs=[pl.BlockSpec(memory_space=pltpu.SMEM), cur(qd), cur(kdd), prev(kdd), cur(kdd), prev(kdd)],
        out_specs=cur(qd),
        compiler_params=_cparams("parallel", "parallel"),
        name="swa_attn",
    )(sinks, q.reshape(bsz, seq, qd), k3, k3, v3, v3)
    return o.reshape(m, qd), w_o.astype(BF16), b_o[None]


def _sgu_body(h_ref, g_ref, win_ref, bin_ref, lng_ref, lnb_ref, ws_ref, bs_ref, wo_ref, bo_ref, o_ref, z_ref, *, width):
    x = h_ref[...]
    tm = x.shape[0]
    xn = _rms(x, g_ref[...]).astype(BF16)
    gelu = lambda t: 0.5 * t * (1.0 + lax.erf(t * (0.5 ** 0.5)))
    v = gelu(_mm(xn, win_ref[:, width:]) + bin_ref[:, width:])
    mu = jnp.mean(v, -1, keepdims=True)
    vc = v - mu
    var = jnp.mean(vc * vc, -1, keepdims=True)
    vn = (vc * lax.rsqrt(var + NORM_EPS) * lng_ref[...] + lnb_ref[...]).astype(BF16)
    gdim = width // SG_GROUPS
    for gi in range(SG_GROUPS):
        cs = slice(gi * gdim, (gi + 1) * gdim)
        u = gelu(_mm(xn, win_ref[:, cs]) + bin_ref[:, cs])
        for q in range(tm // SG_CHUNK):
            rs = slice(q * SG_CHUNK, (q + 1) * SG_CHUNK)
            sv = _mm(ws_ref[gi], vn[rs, cs]) + bs_ref[gi]
            z_ref[rs, cs] = (u[rs] * sv).astype(BF16)
    o_ref[...] = x + _mm(z_ref[...], wo_ref[...]) + bo_ref[...]


def _sgu_layer(h, g_mix, w_in, b_in, ln_g, ln_b, w_s, b_s, w_o, b_o):
    m, d = h.shape
    width = w_o.shape[0]
    causal = jnp.tril(jnp.ones((SG_CHUNK, SG_CHUNK), dtype=bool))
    ws = jnp.where(causal[None], w_s, 0.0).astype(BF16)
    bs = jnp.broadcast_to(b_s[:, :, None], (SG_GROUPS, SG_CHUNK, width // SG_GROUPS))
    args = (h, g_mix, w_in.astype(BF16), b_in[None], ln_g[None], ln_b[None], ws, bs, w_o.astype(BF16), b_o[None])
    tm = TOKEN_TILE
    return pl.pallas_call(
        functools.partial(_sgu_body, width=width),
        out_shape=jax.ShapeDtypeStruct((m, d), F32),
        grid=(m // tm,),
        in_specs=[_rows(tm, d)] + [_full(a.shape) for a in args[1:]],
        out_specs=_rows(tm, d),
        scratch_shapes=[pltpu.VMEM((tm, width), BF16)],
        compiler_params=_cparams("parallel"),
        name="sgu",
    )(*args)


def _gla_body(h_ref, g_ref, win_ref, wa1_ref, wa2_ref, ba_ref, gng_ref, wo_ref, o_ref, st_ref, oc_ref, *, dk, dv):
    @pl.when(pl.program_id(1) == 0)
    def _():
        st_ref[...] = jnp.zeros_like(st_ref)

    x = h_ref[...]
    tm = x.shape[0]
    xn = _rms(x, g_ref[...]).astype(BF16)
    hk, hv = dk // GLA_HEADS, dv // GLA_HEADS
    q_all = _mm(xn, win_ref[:, :dk]) * (hk ** -0.5)
    k_all = _mm(xn, win_ref[:, dk:2 * dk])
    v_all = _mm(xn, win_ref[:, 2 * dk:2 * dk + dv])
    a_low = _mm(xn, wa1_ref[...])
    log_a = jax.nn.log_sigmoid(_mm(a_low.astype(BF16), wa2_ref[...]) + ba_ref[...]) / GLA_TAU
    cl = GLA_CHUNK
    ri = lax.broadcasted_iota(jnp.int32, (cl, cl), 0)
    ci = lax.broadcasted_iota(jnp.int32, (cl, cl), 1)
    causal = ci <= ri
    tri = causal.astype(BF16)
    for c in range(tm // cl):
        rs = slice(c * cl, (c + 1) * cl)
        bcum = _mm_rhs2(tri, log_a[rs])
        b_last = bcum[cl - 1:cl]
        eb = jnp.exp(bcum)
        q_g = q_all[rs] * eb
        k_g = k_all[rs] * jnp.exp(-bcum)
        k_s = k_all[rs] * jnp.exp(b_last - bcum)
        d_last = jnp.exp(b_last)
        for hh in range(GLA_HEADS):
            ks_ = slice(hh * hk, (hh + 1) * hk)
            vs_ = slice(hh * hv, (hh + 1) * hv)
            qh = q_g[:, ks_].astype(BF16)
            vh = v_all[rs, vs_].astype(BF16)
            att = jnp.where(causal, _mm_nt(qh, k_g[:, ks_].astype(BF16)), 0.0)
            st = st_ref[hh]
            oc_ref[rs, vs_] = _mm(att.astype(BF16), vh) + _mm_nt(qh, st.astype(BF16))
            st_ref[hh] = st * d_last[:, ks_] + _mm_tn(vh, k_s[:, ks_].astype(BF16))
    gate = _mm(xn, win_ref[:, 2 * dk + dv:])
    gate = gate * jax.nn.sigmoid(gate)
    for hh in range(GLA_HEADS):
        vs_ = slice(hh * hv, (hh + 1) * hv)
        o = oc_ref[:, vs_]
        o = o * lax.rsqrt(jnp.mean(o * o, -1, keepdims=True) + NORM_EPS) * gng_ref[:, vs_]
        oc_ref[:, vs_] = o * gate[:, vs_]
    o_ref[...] = x + _mm(oc_ref[...].astype(BF16), wo_ref[...])


def _gla_layer(h, g_mix, w_in, w_a2, b_a, gn_g, w_o, bsz, seq):
    m, d = h.shape
    dk = w_a2.shape[1]
    dv = w_o.shape[0]
    lora = w_a2.shape[0]
    main = 2 * dk + 2 * dv
    wa1 = jnp.pad(w_in[:, main:], ((0, 0), (0, LANES - lora))).astype(BF16)
    wa2 = jnp.pad(w_a2, ((0, LANES - lora), (0, 0))).astype(BF16)
    args = (h.reshape(bsz, seq, d), g_mix, w_in[:, :main].astype(BF16), wa1, wa2, b_a[None], gn_g[None], w_o.astype(BF16))
    tm = TOKEN_TILE
    tile = pl.BlockSpec((None, tm, d), lambda b_, i: (b_, i, 0))
    out = pl.pallas_call(
        functools.partial(_gla_body, dk=dk, dv=dv),
        out_shape=jax.ShapeDtypeStruct((bsz, seq, d), F32),
        grid=(bsz, seq // tm),
        in_specs=[tile] + [_full(a.shape) for a in args[1:]],
        out_specs=tile,
        scratch_shapes=[pltpu.VMEM((GLA_HEADS, dv // GLA_HEADS, dk // GLA_HEADS), F32), pltpu.VMEM((tm, dv), F32)],
        compiler_params=_cparams("parallel", "arbitrary"),
        name="gla",
    )(*args)
    return out.reshape(m, d)


def _seg_sum(x, seg):
    return _mm_lhs2(x, seg)


def _seg_bcast(s, seg_t):
    return _mm_lhs2(s, seg_t)


def _rw_proj_body(h_ref, hp_ref, g_ref, mu_ref, wrkv_ref, w0_ref, w1_ref, w2_ref, a0_ref, a1_ref, a2_ref,
                  g1_ref, g2_ref, kk_ref, ka_ref, seg_ref, segt_ref,
                  r_out, ld_out, k_out, v_out, kkn_out, b_out, g_out, *, seq):
    g = g_ref[...]
    xn = _rms(h_ref[...], g)
    tm = xn.shape[0]
    starts_seq = lax.rem(pl.program_id(0) * tm, seq) == 0
    prev_row = _rms(hp_ref[...], g)[SUBLANES - 1:SUBLANES] * jnp.where(starts_seq, 0.0, 1.0)
    row = lax.broadcasted_iota(jnp.int32, xn.shape, 0)
    xx = jnp.where(row == 0, prev_row, pltpu.roll(xn, 1, 0)) - xn
    mix = lambda c: (xn + xx * mu_ref[c:c + 1]).astype(BF16)
    r = _mm(mix(0), wrkv_ref[0])
    k = _mm(mix(2), wrkv_ref[1])
    v = _mm(mix(3), wrkv_ref[2])
    w_pre = w0_ref[...] + _mm(jnp.tanh(_mm(mix(1), w1_ref[...])).astype(BF16), w2_ref[...])
    w = -jax.nn.softplus(-w_pre) - 0.5
    a = jax.nn.sigmoid(a0_ref[...] + _mm(_mm(mix(4), a1_ref[...]).astype(BF16), a2_ref[...]))
    gate = _mm(jax.nn.sigmoid(_mm(mix(5), g1_ref[...])).astype(BF16), g2_ref[...])
    kk = k * kk_ref[...]
    ss = _seg_sum(kk * kk, seg_ref[...])
    kk = kk * _seg_bcast(lax.rsqrt(jnp.maximum(ss, 1e-24)), segt_ref[...])
    r_out[...] = r.astype(BF16)
    ld_out[...] = -jnp.exp(w)
    k_out[...] = (k * (1.0 + (a - 1.0) * ka_ref[...])).astype(BF16)
    v_out[...] = v.astype(BF16)
    kkn_out[...] = kk.astype(BF16)
    b_out[...] = (kk * a).astype(BF16)
    g_out[...] = gate.astype(BF16)


def _stack_heads(z):
    lo = lax.broadcasted_iota(jnp.int32, z.shape, 1) < HEAD_DIM
    zero = jnp.zeros_like(z)
    return jnp.concatenate([jnp.where(lo, z, zero), jnp.where(lo, zero, z)], 0)


def _rw_scan_body(r_ref, ld_ref, k_ref, v_ref, kk_ref, b_ref, gate_ref, rk_ref, gng_ref, gnb_ref, o_ref,
                  st_ref, cum_ref, lhs_ref, rhs_ref, end_ref, vs_ref, bon_ref, pw_ref, t_ref, ak_ref, rbk_ref,
                  av_ref, g_ref, u_ref, wv_ref, yr_ref):
    @pl.when(pl.program_id(1) == 0)
    def _():
        st_ref[...] = jnp.zeros_like(st_ref)

    cl = r_ref.shape[0]
    d = r_ref.shape[1]
    pairs = range(d // LANES)
    c2 = 2 * cl
    bf = lambda t: t.astype(BF16)
    ri = lax.broadcasted_iota(jnp.int32, (cl, cl), 0)
    ci = lax.broadcasted_iota(jnp.int32, (cl, cl), 1)
    tri = (ci <= ri).astype(BF16)
    ld_all = ld_ref[...]
    hi, lo = _split(ld_all)
    lo2 = (ld_all - hi.astype(F32) - lo.astype(F32)).astype(BF16)
    cum_ref[...] = _mm(tri, hi) + (_mm(tri, lo) + _mm(tri, lo2))

    for p in pairs:
        cs = slice(p * LANES, (p + 1) * LANES)
        cum = cum_ref[:, cs]
        cum_end = cum_ref[cl - 1:cl, cs]
        p_inv = jnp.exp(-cum)
        to_end = jnp.exp(cum_end - cum)
        r = r_ref[:, cs].astype(F32)
        k = k_ref[:, cs].astype(F32)
        kk = kk_ref[:, cs].astype(F32)
        b = b_ref[:, cs].astype(F32)
        vs = _stack_heads(v_ref[:, cs])
        lhs_ref[p] = jnp.concatenate([_stack_heads(bf(kk * jnp.exp(cum - ld_ref[:, cs]))),
                                      _stack_heads(bf(r * jnp.exp(cum)))], 0)
        rhs_ref[p] = jnp.concatenate([_stack_heads(bf(b * p_inv)), _stack_heads(bf(k * p_inv))], 0)
        end_ref[p] = jnp.concatenate([_stack_heads(bf(b * to_end)), _stack_heads(bf(k * to_end))], 0)
        vs_ref[p] = vs
        bonus = jnp.sum(_stack_heads(r * k * rk_ref[:, cs]), -1, keepdims=True) * vs.astype(F32)
        bon_ref[p] = bonus[:cl] + bonus[cl:]

    r2 = lax.broadcasted_iota(jnp.int32, (c2, c2), 0)
    q2 = lax.broadcasted_iota(jnp.int32, (c2, c2), 1)
    same = (r2 < cl) == (q2 < cl)
    strict = same & (q2 < r2)
    incl = same & (q2 <= r2)
    eye = (r2 == q2).astype(F32)
    for p in pairs:
        a_all = _mm_nt(lhs_ref[p], rhs_ref[p])
        a_ab = jnp.where(strict, a_all[:c2, :c2], 0.0)
        pw_ref[p] = bf(-a_ab)
        t_ref[p] = eye - a_ab
        ak_ref[p] = bf(jnp.where(strict, a_all[:c2, c2:], 0.0))
        rbk_ref[p, :, :c2] = bf(jnp.where(incl, a_all[c2:, :c2], 0.0))
        rbk_ref[p, :, c2:] = bf(jnp.where(incl, a_all[c2:, c2:], 0.0))

    for _ in range(cl.bit_length() - 2):
        for p in pairs:
            pw = pw_ref[p]
            pw_ref[p] = bf(_mm(pw, pw))
        for p in pairs:
            t_inv = t_ref[p]
            t_ref[p] = t_inv + _mm(bf(t_inv), pw_ref[p])
    for p in pairs:
        av_ref[p] = bf(_mm(ak_ref[p], vs_ref[p]))
    for p in pairs:
        gu = _mm(bf(t_ref[p]), jnp.concatenate([lhs_ref[p, :c2], av_ref[p]], 1))
        g_ref[p] = bf(gu[:, :LANES])
        u_ref[p] = gu[:, LANES:]

    for p in pairs:
        gr = _mm_nt(jnp.concatenate([g_ref[p], lhs_ref[p, c2:]], 0), bf(st_ref[p]))
        wv_ref[p, :c2] = bf(-(gr[:c2] + u_ref[p]))
        wv_ref[p, c2:] = vs_ref[p]
        yr_ref[p] = gr[c2:]
    inv_n = 1.0 / HEAD_DIM
    own = lax.broadcasted_iota(jnp.int32, (c2, LANES), 1) // HEAD_DIM == lax.broadcasted_iota(jnp.int32, (c2, LANES), 0) // cl
    for p in pairs:
        cs = slice(p * LANES, (p + 1) * LANES)
        wv = wv_ref[p]
        ys = yr_ref[p] + _mm(rbk_ref[p], wv)
        st_ref[p] = st_ref[p] * jnp.exp(cum_ref[cl - 1:cl, cs]) + _mm_tn(wv, end_ref[p])
        yc = jnp.where(own, ys - jnp.sum(ys, -1, keepdims=True) * inv_n, 0.0)
        yn = yc * lax.rsqrt(jnp.sum(yc * yc, -1, keepdims=True) * inv_n + RW_GN_EPS)
        out = (yn[:cl] + yn[cl:]) * gng_ref[:, cs] + gnb_ref[:, cs] + bon_ref[p]
        o_ref[:, cs] = (out * gate_ref[:, cs].astype(F32)).astype(BF16)


def _rwkv_layer(h, g_mix, mu, w_rkv, w0, w1, w2, a0, a1, a2, g1, g2, k_k, k_a, r_k, gn_g, gn_b, w_o, bsz, seq):
    m, d = h.shape
    seg = (jnp.arange(d)[:, None] // HEAD_DIM == jnp.arange(LANES)[None, :]).astype(BF16)
    segt = seg.T
    row = lambda t: t.reshape(1, -1)
    tm = TOKEN_TILE
    proj_args = (h, h, g_mix, mu, w_rkv.astype(BF16), row(w0), w1.astype(BF16), w2.astype(BF16), row(a0),
                 a1.astype(BF16), a2.astype(BF16), g1.astype(BF16), g2.astype(BF16), row(k_k), row(k_a), seg, segt)
    before = pl.BlockSpec((SUBLANES, d), lambda i: (jnp.maximum(i * (tm // SUBLANES) - 1, 0), 0))
    dts = (BF16, F32, BF16, BF16, BF16, BF16, BF16)
    outs = pl.pallas_call(
        functools.partial(_rw_proj_body, seq=seq),
        out_shape=tuple(jax.ShapeDtypeStruct((m, d), dt) for dt in dts),
        grid=(m // tm,),
        in_specs=[_rows(tm, d), before] + [_full(a.shape) for a in proj_args[2:]],
        out_specs=tuple(_rows(tm, d) for _ in dts),
        compiler_params=_cparams("parallel"),
        name="rwkv_proj",
    )(*proj_args)
    cl = RW_CHUNK
    c2 = 2 * cl
    npair = d // LANES
    blk = pl.BlockSpec((None, cl, d), lambda b_, i: (b_, i, 0))
    vec = (row(r_k), row(gn_g), row(gn_b))
    o = pl.pallas_call(
        _rw_scan_body,
        out_shape=jax.ShapeDtypeStruct((bsz, seq, d), BF16),
        grid=(bsz, seq // cl),
        in_specs=[blk] * 7 + [pl.BlockSpec(a.shape, lambda b_, i: (0, 0)) for a in vec],
        out_specs=blk,
        scratch_shapes=[
            pltpu.VMEM((npair, c2, LANES), F32),
            pltpu.VMEM((cl, d), F32),
            pltpu.VMEM((npair, 2 * c2, LANES), BF16),
            pltpu.VMEM((npair, 2 * c2, LANES), BF16),
            pltpu.VMEM((npair, 2 * c2, LANES), BF16),
            pltpu.VMEM((npair, c2, LANES), BF16),
            pltpu.VMEM((npair, cl, LANES), F32),
            pltpu.VMEM((npair, c2, c2), BF16),
            pltpu.VMEM((npair, c2, c2), F32),
            pltpu.VMEM((npair, c2, c2), BF16),
            pltpu.VMEM((npair, c2, 2 * c2), BF16),
            pltpu.VMEM((npair, c2, LANES), BF16),
            pltpu.VMEM((npair, c2, LANES), BF16),
            pltpu.VMEM((npair, c2, LANES), F32),
            pltpu.VMEM((npair, 2 * c2, LANES), BF16),
            pltpu.VMEM((npair, c2, LANES), F32),
        ],
        compiler_params=_cparams("parallel", "arbitrary"),
        name="rwkv_scan",
    )(*[t.reshape(bsz, seq, d) for t in outs], *vec)
    return o.reshape(m, d), w_o.astype(BF16), jnp.zeros((1, d), F32)


def kernel(x, positions, norm_mix, norm_ffn, ffn_w_in, ffn_w_out, norm_final, rw_mu, rw_w_rkv, rw_w0, rw_w1, rw_w2, rw_a0, rw_a1, rw_a2, rw_g1, rw_g2, rw_k_k, rw_k_a, rw_r_k, rw_gn_g, rw_gn_b, rw_w_o, sw_w_qkv, sw_b_qkv, sw_sinks, sw_w_o, sw_b_o, sg_w_in, sg_b_in, sg_ln_g, sg_ln_b, sg_w_s, sg_b_s, sg_w_o, sg_b_o, gla_w_in, gla_w_a2, gla_b_a, gla_gn_g, gla_w_o):
    bsz, seq, d = x.shape
    depth = norm_mix.shape[0]
    h = x.reshape(bsz * seq, d)
    g_final = norm_final[None]
    for i in range(depth):
        t, j = i % 4, i // 4
        g_mix = norm_mix[i][None]
        pre = None
        if t == 0:
            pre = _rwkv_layer(h, g_mix, rw_mu[j], rw_w_rkv[j], rw_w0[j], rw_w1[j], rw_w2[j], rw_a0[j], rw_a1[j],
                            rw_a2[j], rw_g1[j], rw_g2[j], rw_k_k[j], rw_k_a[j], rw_r_k[j], rw_gn_g[j], rw_gn_b[j],
                            rw_w_o[j], bsz, seq)
        elif t == 1:
            pre = _swa_layer(h, positions, g_mix, sw_w_qkv[j], sw_b_qkv[j], sw_sinks[j], sw_w_o[j], sw_b_o[j], bsz, seq)
        elif t == 2:
            h = _sgu_layer(h, g_mix, sg_w_in[j], sg_b_in[j], sg_ln_g[j], sg_ln_b[j], sg_w_s[j], sg_b_s[j],
                           sg_w_o[j], sg_b_o[j])
        else:
            h = _gla_layer(h, g_mix, gla_w_in[j], gla_w_a2[j], gla_b_a[j], gla_gn_g[j], gla_w_o[j], bsz, seq)
        h = _ffn(h, norm_ffn[i][None], ffn_w_in[i].astype(BF16), ffn_w_out[i].astype(BF16), g_final,
                 pre=pre, final=i == depth - 1)
    return h.reshape(bsz, seq, d)
```

```python
import functools

import jax
import jax.numpy as jnp
from jax import lax
from jax.experimental import pallas as pl
from jax.experimental.pallas import tpu as pltpu

F32 = jnp.float32
BF16 = jnp.bfloat16

HEAD_DIM = 64
NORM_EPS = 1e-5
RW_GN_EPS = 64e-5
SW_Q_HEADS = 16
SW_KV_HEADS = 2
SW_BLOCK = 128
ROPE_THETA = 10000.0
SG_CHUNK = 128
SG_GROUPS = 16
GLA_HEADS = 4
GLA_TAU = 16.0
GLA_CHUNK = 64
RW_CHUNK = 64

MXU_COLS = 256
LANES = 128
SUBLANES = 8
TOKEN_TILE = 512
VMEM_LIMIT = 56 * 1024 * 1024


def _mm(a, b):
    return jnp.dot(a, b, preferred_element_type=F32)


def _mm_nt(a, b):
    return lax.dot_general(a, b, (((1,), (1,)), ((), ())), preferred_element_type=F32)


def _mm_tn(a, b):
    return lax.dot_general(a, b, (((0,), (0,)), ((), ())), preferred_element_type=F32)


def _split(x):
    hi = x.astype(BF16)
    lo = (x - hi.astype(F32)).astype(BF16)
    return hi, lo


def _mm_rhs2(a, b):
    hi, lo = _split(b)
    return _mm(a, hi) + _mm(a, lo)


def _rms(x, g):
    return x * lax.rsqrt(jnp.mean(x * x, -1, keepdims=True) + NORM_EPS) * g


def _cparams(*sem):
    return pltpu.CompilerParams(dimension_semantics=sem, vmem_limit_bytes=VMEM_LIMIT)


def _full(shape):
    nd = len(shape)
    return pl.BlockSpec(shape, lambda *_: (0,) * nd)


def _rows(tm, n):
    return pl.BlockSpec((tm, n), lambda i: (i, 0))


def _ffn_body(*refs, hidden, fc, pre, final):
    if pre:
        h_ref, a_ref, wo_ref, bo_ref, g_ref, win_ref, wout_ref, gf_ref, o_ref = refs
        x = h_ref[...] + _mm(a_ref[...], wo_ref[...]) + bo_ref[...]
    else:
        h_ref, g_ref, win_ref, wout_ref, gf_ref, o_ref = refs
        x = h_ref[...]
    xn = _rms(x, g_ref[...]).astype(BF16)
    acc = jnp.zeros_like(x)
    for c in range(hidden // fc):
        gate = _mm(xn, win_ref[:, c * fc:(c + 1) * fc])
        up = _mm(xn, win_ref[:, hidden + c * fc:hidden + (c + 1) * fc])
        act = (gate * jax.nn.sigmoid(gate) * up).astype(BF16)
        acc = acc + _mm(act, wout_ref[c * fc:(c + 1) * fc, :])
    y = x + acc
    if final:
        y = _rms(y, gf_ref[...])
    o_ref[...] = y


def _ffn(h, g, w_in, w_out, g_final, layer, *, pre=None, final=False):
    m, d = h.shape
    hidden = w_out.shape[1]
    tm = TOKEN_TILE
    args, specs = [h], [_rows(tm, d)]
    if pre is not None:
        a, w_o, b_o = pre
        args += [a, w_o, b_o]
        specs += [_rows(tm, a.shape[1]), _full(w_o.shape), _full(b_o.shape)]
    args += [g, w_in, w_out, g_final]
    of_layer = lambda w: pl.BlockSpec((None,) + w.shape[1:], lambda i: (layer, 0, 0))
    specs += [_full(g.shape), of_layer(w_in), of_layer(w_out), _full(g_final.shape)]
    return pl.pallas_call(
        functools.partial(_ffn_body, hidden=hidden, fc=MXU_COLS, pre=pre is not None, final=final),
        out_shape=jax.ShapeDtypeStruct((m, d), F32),
        grid=(m // tm,),
        in_specs=specs,
        out_specs=_rows(tm, d),
        compiler_params=_cparams("parallel"),
        name="ffn",
    )(*args)


def _swa_qkv_body(h_ref, g_ref, w_ref, b_ref, cos_ref, sin_ref, q_ref, k_ref, v_ref, *, qd):
    xn = _rms(h_ref[...], g_ref[...]).astype(BF16)
    qkv = _mm(xn, w_ref[...]) + b_ref[...]
    cos = cos_ref[...]
    sin = sin_ref[...]
    lane = lax.broadcasted_iota(jnp.int32, cos.shape, 1)
    first_half = (lane & (HEAD_DIM - 1)) < HEAD_DIM // 2

    def rope(t):
        rot = jnp.where(first_half, pltpu.roll(t, LANES - HEAD_DIM // 2, 1), pltpu.roll(t, HEAD_DIM // 2, 1))
        return t * cos + rot * sin

    scale = HEAD_DIM ** -0.5
    for j in range(qd // LANES):
        q_ref[:, j * LANES:(j + 1) * LANES] = (rope(qkv[:, j * LANES:(j + 1) * LANES]) * scale).astype(BF16)
    kd = k_ref.shape[1]
    for j in range(kd // LANES):
        k_ref[:, j * LANES:(j + 1) * LANES] = rope(qkv[:, qd + j * LANES:qd + (j + 1) * LANES]).astype(BF16)
    v_ref[...] = qkv[:, qd + kd:].astype(BF16)


def _swa_attn_body(sink_ref, q_ref, kc_ref, kp_ref, vc_ref, vp_ref, o_ref):
    n = pl.program_id(1)
    blk = SW_BLOCK
    kcat = jnp.concatenate([kp_ref[...], kc_ref[...]], 0)
    vcat = jnp.concatenate([vp_ref[...], vc_ref[...]], 0)
    r = lax.broadcasted_iota(jnp.int32, (blk, 2 * blk), 0)
    c = lax.broadcasted_iota(jnp.int32, (blk, 2 * blk), 1)
    no_prev = jnp.where(n > 0, 0, 2 * blk)
    valid = jnp.where(c < blk, c - r - no_prev - 1, r - c + blk) >= 0
    lo = lax.broadcasted_iota(jnp.int32, (2 * blk, LANES), 1) < HEAD_DIM
    group = SW_Q_HEADS // SW_KV_HEADS
    for j in range(SW_KV_HEADS):
        kj = kcat[:, j * LANES:(j + 1) * LANES]
        vj = vcat[:, j * LANES:(j + 1) * LANES]
        zero = jnp.zeros_like(kj)
        ke = (jnp.where(lo, kj, zero), jnp.where(lo, zero, kj))
        ve = (jnp.where(lo, vj, zero), jnp.where(lo, zero, vj))
        for p in range(group // 2):
            col = (j * (group // 2) + p) * LANES
            qp = q_ref[:, col:col + LANES]
            acc = jnp.zeros((blk, LANES), F32)
            for e in range(2):
                sink = sink_ref[j * group + 2 * p + e]
                s = jnp.where(valid, _mm_nt(qp, ke[e]), -jnp.inf)
                mx = jnp.maximum(jnp.max(s, -1, keepdims=True), sink)
                pe = jnp.exp(s - mx)
                den = jnp.sum(pe, -1, keepdims=True) + jnp.exp(sink - mx)
                acc = acc + _mm(pe.astype(BF16), ve[e]) / den
            o_ref[:, col:col + LANES] = acc.astype(BF16)


def _swa_layer(h, positions, g_mix, w_qkv, b_qkv, sinks, w_o, b_o, bsz, seq):
    m, d = h.shape
    qd = SW_Q_HEADS * HEAD_DIM
    kd = SW_KV_HEADS * HEAD_DIM
    heads = [slice(base + j * HEAD_DIM, base + (j + 1) * HEAD_DIM)
             for base in (qd, qd + kd) for j in range(SW_KV_HEADS) for _ in (0, 1)]
    w = jnp.concatenate([w_qkv[:, :qd]] + [w_qkv[:, s] for s in heads], 1).astype(BF16)
    b = jnp.concatenate([b_qkv[:qd]] + [b_qkv[s] for s in heads])[None]
    inv_freq = ROPE_THETA ** (-jnp.arange(0, HEAD_DIM, 2, dtype=F32) / HEAD_DIM)
    ang = positions.astype(F32).reshape(m, 1) * inv_freq
    cos = jnp.tile(jnp.cos(ang), (1, 4))
    sin = jnp.tile(jnp.concatenate([-jnp.sin(ang), jnp.sin(ang)], -1), (1, 2))
    tm = TOKEN_TILE
    kdd = 2 * kd
    q, k, v = pl.pallas_call(
        functools.partial(_swa_qkv_body, qd=qd),
        out_shape=(jax.ShapeDtypeStruct((m, qd), BF16), jax.ShapeDtypeStruct((m, kdd), BF16),
                   jax.ShapeDtypeStruct((m, kdd), BF16)),
        grid=(m // tm,),
        in_specs=[_rows(tm, d), _full(g_mix.shape), _full(w.shape), _full(b.shape), _rows(tm, LANES), _rows(tm, LANES)],
        out_specs=(_rows(tm, qd), _rows(tm, kdd), _rows(tm, kdd)),
        compiler_params=_cparams("parallel"),
        name="swa_qkv",
    )(h, g_mix, w, b, cos, sin)
    nb = seq // SW_BLOCK
    cur = lambda n_: pl.BlockSpec((None, SW_BLOCK, n_), lambda b_, i: (b_, i, 0))
    prev = lambda n_: pl.BlockSpec((None, SW_BLOCK, n_), lambda b_, i: (b_, jnp.maximum(i - 1, 0), 0))
    k3 = k.reshape(bsz, seq, kdd)
    v3 = v.reshape(bsz, seq, kdd)
    o = pl.pallas_call(
        _swa_attn_body,
        out_shape=jax.ShapeDtypeStruct((bsz, seq, qd), BF16),
        grid=(bsz, nb),
        in_specs=[pl.BlockSpec(memory_space=pltpu.SMEM), cur(qd), cur(kdd), prev(kdd), cur(kdd), prev(kdd)],
        out_specs=cur(qd),
        compiler_params=_cparams("parallel", "parallel"),
        name="swa_attn",
    )(sinks, q.reshape(bsz, seq, qd), k3, k3, v3, v3)
    return o.reshape(m, qd), w_o.astype(BF16), b_o[None]


def _sgu_body(h_ref, g_ref, win_ref, bin_ref, lng_ref, lnb_ref, ws_ref, bs_ref, wo_ref, bo_ref, o_ref, z_ref, vn_ref, *, width):
    x = h_ref[...]
    tm = x.shape[0]
    xn = _rms(x, g_ref[...]).astype(BF16)
    gelu = lambda t: 0.5 * t * (1.0 + lax.erf(t * (0.5 ** 0.5)))
    v = gelu(_mm(xn, win_ref[:, width:]) + bin_ref[:, width:])
    mu = jnp.mean(v, -1, keepdims=True)
    vc = v - mu
    var = jnp.mean(vc * vc, -1, keepdims=True)
    vn_ref[...] = (vc * lax.rsqrt(var + NORM_EPS) * lng_ref[...] + lnb_ref[...]).astype(BF16)
    gdim = width // SG_GROUPS
    for gp in range(width // MXU_COLS):
        cp = slice(gp * MXU_COLS, (gp + 1) * MXU_COLS)
        u = gelu(_mm(xn, win_ref[:, cp]) + bin_ref[:, cp])
        for half in range(MXU_COLS // gdim):
            gi = gp * (MXU_COLS // gdim) + half
            cs = slice(gi * gdim, (gi + 1) * gdim)
            us = slice(half * gdim, (half + 1) * gdim)
            for q in range(0, tm // SG_CHUNK, 2):
                r0 = slice(q * SG_CHUNK, (q + 1) * SG_CHUNK)
                r1 = slice((q + 1) * SG_CHUNK, (q + 2) * SG_CHUNK)
                sv = _mm(ws_ref[gi], jnp.concatenate([vn_ref[r0, cs], vn_ref[r1, cs]], 1))
                z_ref[r0, cs] = (u[r0, us] * (sv[:, :gdim] + bs_ref[gi])).astype(BF16)
                z_ref[r1, cs] = (u[r1, us] * (sv[:, gdim:] + bs_ref[gi])).astype(BF16)
    o_ref[...] = x + _mm(z_ref[...], wo_ref[...]) + bo_ref[...]


def _sgu_layer(h, g_mix, w_in, b_in, ln_g, ln_b, w_s, b_s, w_o, b_o):
    m, d = h.shape
    width = w_o.shape[0]
    causal = jnp.tril(jnp.ones((SG_CHUNK, SG_CHUNK), dtype=bool))
    ws = jnp.where(causal[None], w_s, 0.0).astype(BF16)
    bs = jnp.broadcast_to(b_s[:, :, None], (SG_GROUPS, SG_CHUNK, width // SG_GROUPS))
    args = (h, g_mix, w_in.astype(BF16), b_in[None], ln_g[None], ln_b[None], ws, bs, w_o.astype(BF16), b_o[None])
    tm = TOKEN_TILE
    return pl.pallas_call(
        functools.partial(_sgu_body, width=width),
        out_shape=jax.ShapeDtypeStruct((m, d), F32),
        grid=(m // tm,),
        in_specs=[_rows(tm, d)] + [_full(a.shape) for a in args[1:]],
        out_specs=_rows(tm, d),
        scratch_shapes=[pltpu.VMEM((tm, width), BF16), pltpu.VMEM((tm, width), BF16)],
        compiler_params=_cparams("parallel"),
        name="sgu",
    )(*args)


def _gla_body(h_ref, g_ref, win_ref, wa1_ref, wa2_ref, ba_ref, gng_ref, wo_ref, o_ref,
              st_ref, oc_ref, qg_ref, kg_ref, ks_ref, v_ref, dl_ref, kv_ref, *, dk, dv):
    @pl.when(pl.program_id(1) == 0)
    def _():
        st_ref[...] = jnp.zeros_like(st_ref)

    x = h_ref[...]
    tm = x.shape[0]
    xn = _rms(x, g_ref[...]).astype(BF16)
    hk, hv = dk // GLA_HEADS, dv // GLA_HEADS
    q_all = _mm(xn, win_ref[:, :dk]) * (hk ** -0.5)
    k_all = _mm(xn, win_ref[:, dk:2 * dk])
    v_ref[...] = _mm(xn, win_ref[:, 2 * dk:2 * dk + dv]).astype(BF16)
    a_low = _mm(xn, wa1_ref[...])
    log_a = jax.nn.log_sigmoid(_mm(a_low.astype(BF16), wa2_ref[...]) + ba_ref[...]) / GLA_TAU
    cl = GLA_CHUNK
    chunks = [slice(c * cl, (c + 1) * cl) for c in range(tm // cl)]
    heads = [(slice(hh * hk, (hh + 1) * hk), slice(hh * hv, (hh + 1) * hv)) for hh in range(GLA_HEADS)]
    ri = lax.broadcasted_iota(jnp.int32, (cl, cl), 0)
    ci = lax.broadcasted_iota(jnp.int32, (cl, cl), 1)
    causal = ci <= ri
    tri = causal.astype(BF16)
    for c, rs in enumerate(chunks):
        bcum = _mm_rhs2(tri, log_a[rs])
        b_last = bcum[cl - 1:cl]
        qg_ref[rs] = (q_all[rs] * jnp.exp(bcum)).astype(BF16)
        kg_ref[rs] = (k_all[rs] * jnp.exp(-bcum)).astype(BF16)
        ks_ref[rs] = (k_all[rs] * jnp.exp(b_last - bcum)).astype(BF16)
        dl_ref[c:c + 1] = jnp.exp(b_last)
    for c, rs in enumerate(chunks):
        for hh, (ks_, vs_) in enumerate(heads):
            vh = v_ref[rs, vs_]
            att = jnp.where(causal, _mm_nt(qg_ref[rs, ks_], kg_ref[rs, ks_]), 0.0)
            oc_ref[rs, vs_] = _mm(att.astype(BF16), vh)
            kv_ref[c, hh] = _mm_tn(vh, ks_ref[rs, ks_])
    for c, rs in enumerate(chunks):
        for hh, (ks_, vs_) in enumerate(heads):
            st = st_ref[hh]
            oc_ref[rs, vs_] += _mm_nt(qg_ref[rs, ks_], st.astype(BF16))
            st_ref[hh] = st * dl_ref[c:c + 1, ks_] + kv_ref[c, hh]
    gate = _mm(xn, win_ref[:, 2 * dk + dv:])
    gate = gate * jax.nn.sigmoid(gate)
    for hh in range(GLA_HEADS):
        vs_ = slice(hh * hv, (hh + 1) * hv)
        o = oc_ref[:, vs_]
        o = o * lax.rsqrt(jnp.mean(o * o, -1, keepdims=True) + NORM_EPS) * gng_ref[:, vs_]
        oc_ref[:, vs_] = o * gate[:, vs_]
    o_ref[...] = x + _mm(oc_ref[...].astype(BF16), wo_ref[...])


def _gla_layer(h, g_mix, w_in, w_a2, b_a, gn_g, w_o, bsz, seq):
    m, d = h.shape
    dk = w_a2.shape[1]
    dv = w_o.shape[0]
    lora = w_a2.shape[0]
    main = 2 * dk + 2 * dv
    wa1 = jnp.pad(w_in[:, main:], ((0, 0), (0, LANES - lora))).astype(BF16)
    wa2 = jnp.pad(w_a2, ((0, LANES - lora), (0, 0))).astype(BF16)
    args = (h.reshape(bsz, seq, d), g_mix, w_in[:, :main].astype(BF16), wa1, wa2, b_a[None], gn_g[None], w_o.astype(BF16))
    tm = TOKEN_TILE
    hk, hv = dk // GLA_HEADS, dv // GLA_HEADS
    tile = pl.BlockSpec((None, tm, d), lambda b_, i: (b_, i, 0))
    out = pl.pallas_call(
        functools.partial(_gla_body, dk=dk, dv=dv),
        out_shape=jax.ShapeDtypeStruct((bsz, seq, d), F32),
        grid=(bsz, seq // tm),
        in_specs=[tile] + [_full(a.shape) for a in args[1:]],
        out_specs=tile,
        scratch_shapes=[
            pltpu.VMEM((GLA_HEADS, hv, hk), F32),
            pltpu.VMEM((tm, dv), F32),
            pltpu.VMEM((tm, dk), BF16),
            pltpu.VMEM((tm, dk), BF16),
            pltpu.VMEM((tm, dk), BF16),
            pltpu.VMEM((tm, dv), BF16),
            pltpu.VMEM((tm // GLA_CHUNK, dk), F32),
            pltpu.VMEM((tm // GLA_CHUNK, GLA_HEADS, hv, hk), F32),
        ],
        compiler_params=_cparams("parallel", "arbitrary"),
        name="gla",
    )(*args)
    return out.reshape(m, d)


def _rw_proj_body(h_ref, hp_ref, g_ref, mu_ref, wrkv_ref, w0_ref, w1_ref, w2_ref, a0_ref, a1_ref, a2_ref,
                  g1_ref, g2_ref, kk_ref, ka_ref,
                  r_out, ld_out, k_out, v_out, kk_out, a_out, g_out, *, seq):
    g = g_ref[...]
    xn = _rms(h_ref[...], g)
    tm = xn.shape[0]
    starts_seq = lax.rem(pl.program_id(0) * tm, seq) == 0
    prev_row = _rms(hp_ref[...], g)[SUBLANES - 1:SUBLANES] * jnp.where(starts_seq, 0.0, 1.0)
    row = lax.broadcasted_iota(jnp.int32, xn.shape, 0)
    xx = jnp.where(row == 0, prev_row, pltpu.roll(xn, 1, 0)) - xn
    mix = lambda c: (xn + xx * mu_ref[c:c + 1]).astype(BF16)
    r = _mm(mix(0), wrkv_ref[0])
    k = _mm(mix(2), wrkv_ref[1])
    v = _mm(mix(3), wrkv_ref[2])
    w_pre = w0_ref[...] + _mm(jnp.tanh(_mm(mix(1), w1_ref[...])).astype(BF16), w2_ref[...])
    w = -jax.nn.softplus(-w_pre) - 0.5
    a = jax.nn.sigmoid(a0_ref[...] + _mm(_mm(mix(4), a1_ref[...]).astype(BF16), a2_ref[...]))
    gate = _mm(jax.nn.sigmoid(_mm(mix(5), g1_ref[...])).astype(BF16), g2_ref[...])
    r_out[...] = r.astype(BF16)
    ld_out[...] = -jnp.exp(w)
    k_out[...] = (k * (1.0 + (a - 1.0) * ka_ref[...])).astype(BF16)
    v_out[...] = v.astype(BF16)
    kk_out[...] = (k * kk_ref[...]).astype(BF16)
    a_out[...] = a.astype(BF16)
    g_out[...] = gate.astype(BF16)


def _stack_heads(z):
    lo = lax.broadcasted_iota(jnp.int32, z.shape, 1) < HEAD_DIM
    zero = jnp.zeros_like(z)
    return jnp.concatenate([jnp.where(lo, z, zero), jnp.where(lo, zero, z)], 0)


def _rw_scan_body(r_ref, ld_ref, k_ref, v_ref, kk_ref, a_ref, gate_ref, rk_ref, gng_ref, gnb_ref, o_ref,
                  st_ref, cum_ref, lhs_ref, rhs_ref, end_ref, vs_ref, bon_ref, pw_ref, t_ref, ak_ref, rbk_ref,
                  av_ref, g_ref, u_ref, wv_ref, yr_ref):
    @pl.when(pl.program_id(1) == 0)
    def _():
        st_ref[...] = jnp.zeros_like(st_ref)

    cl = r_ref.shape[0]
    d = r_ref.shape[1]
    pairs = range(d // LANES)
    c2 = 2 * cl
    bf = lambda t: t.astype(BF16)
    ri = lax.broadcasted_iota(jnp.int32, (cl, cl), 0)
    ci = lax.broadcasted_iota(jnp.int32, (cl, cl), 1)
    tri = (ci <= ri).astype(BF16)
    ld_all = ld_ref[...]
    hi, lo = _split(ld_all)
    lo2 = (ld_all - hi.astype(F32) - lo.astype(F32)).astype(BF16)
    cum_ref[...] = _mm(tri, hi) + (_mm(tri, lo) + _mm(tri, lo2))

    for p in pairs:
        cs = slice(p * LANES, (p + 1) * LANES)
        cum = cum_ref[:, cs]
        cum_end = cum_ref[cl - 1:cl, cs]
        p_inv = jnp.exp(-cum)
        to_end = jnp.exp(cum_end - cum)
        r = r_ref[:, cs].astype(F32)
        k = k_ref[:, cs].astype(F32)
        kk = kk_ref[:, cs].astype(F32)
        kk_st = _stack_heads(kk)
        inv = lax.rsqrt(jnp.maximum(jnp.sum(kk_st * kk_st, -1, keepdims=True), 1e-24))
        kk = kk * jnp.where(lax.broadcasted_iota(jnp.int32, kk.shape, 1) < HEAD_DIM, inv[:cl], inv[cl:])
        b = kk * a_ref[:, cs].astype(F32)
        vs = _stack_heads(v_ref[:, cs])
        lhs_ref[p] = jnp.concatenate([_stack_heads(bf(kk * jnp.exp(cum - ld_ref[:, cs]))),
                                      _stack_heads(bf(r * jnp.exp(cum)))], 0)
        rhs_ref[p] = jnp.concatenate([_stack_heads(bf(b * p_inv)), _stack_heads(bf(k * p_inv))], 0)
        end_ref[p] = jnp.concatenate([_stack_heads(bf(b * to_end)), _stack_heads(bf(k * to_end))], 0)
        vs_ref[p] = vs
        bonus = jnp.sum(_stack_heads(r * k * rk_ref[:, cs]), -1, keepdims=True) * vs.astype(F32)
        bon_ref[p] = bonus[:cl] + bonus[cl:]

    r2 = lax.broadcasted_iota(jnp.int32, (c2, c2), 0)
    q2 = lax.broadcasted_iota(jnp.int32, (c2, c2), 1)
    same = (r2 < cl) == (q2 < cl)
    strict = same & (q2 < r2)
    incl = same & (q2 <= r2)
    eye = (r2 == q2).astype(F32)
    for p in pairs:
        a_all = _mm_nt(lhs_ref[p], rhs_ref[p])
        a_ab = jnp.where(strict, a_all[:c2, :c2], 0.0)
        pw_ref[p] = bf(-a_ab)
        t_ref[p] = eye - a_ab
        ak_ref[p] = bf(jnp.where(strict, a_all[:c2, c2:], 0.0))
        rbk_ref[p, :, :c2] = bf(jnp.where(incl, a_all[c2:, :c2], 0.0))
        rbk_ref[p, :, c2:] = bf(jnp.where(incl, a_all[c2:, c2:], 0.0))

    for p in pairs:
        pw = pw_ref[p]
        pw_ref[p] = bf(_mm(pw, pw))
    for _ in range(cl.bit_length() - 3):
        for p in pairs:
            pw = pw_ref[p]
            t_inv = t_ref[p]
            both = _mm(pw, jnp.concatenate([pw, bf(t_inv)], 1))
            pw_ref[p] = bf(both[:, :c2])
            t_ref[p] = t_inv + both[:, c2:]
    for p in pairs:
        t_inv = t_ref[p]
        t_ref[p] = t_inv + _mm(pw_ref[p], bf(t_inv))
    for p in pairs:
        av_ref[p] = bf(_mm(ak_ref[p], vs_ref[p]))
    for p in pairs:
        gu = _mm(bf(t_ref[p]), jnp.concatenate([lhs_ref[p, :c2], av_ref[p]], 1))
        g_ref[p] = bf(gu[:, :LANES])
        u_ref[p] = gu[:, LANES:]

    for p in pairs:
        gr = _mm_nt(jnp.concatenate([g_ref[p], lhs_ref[p, c2:]], 0), bf(st_ref[p]))
        wv_ref[p, :c2] = bf(-(gr[:c2] + u_ref[p]))
        wv_ref[p, c2:] = vs_ref[p]
        yr_ref[p] = gr[c2:]
    inv_n = 1.0 / HEAD_DIM
    own = lax.broadcasted_iota(jnp.int32, (c2, LANES), 1) // HEAD_DIM == lax.broadcasted_iota(jnp.int32, (c2, LANES), 0) // cl
    for p in pairs:
        cs = slice(p * LANES, (p + 1) * LANES)
        wv = wv_ref[p]
        ys = yr_ref[p] + _mm(rbk_ref[p], wv)
        st_ref[p] = st_ref[p] * jnp.exp(cum_ref[cl - 1:cl, cs]) + _mm_tn(wv, end_ref[p])
        yc = jnp.where(own, ys - jnp.sum(ys, -1, keepdims=True) * inv_n, 0.0)
        yn = yc * lax.rsqrt(jnp.sum(yc * yc, -1, keepdims=True) * inv_n + RW_GN_EPS)
        out = (yn[:cl] + yn[cl:]) * gng_ref[:, cs] + gnb_ref[:, cs] + bon_ref[p]
        o_ref[:, cs] = (out * gate_ref[:, cs].astype(F32)).astype(BF16)


def _rwkv_layer(h, g_mix, mu, w_rkv, w0, w1, w2, a0, a1, a2, g1, g2, k_k, k_a, r_k, gn_g, gn_b, w_o, bsz, seq):
    m, d = h.shape
    row = lambda t: t.reshape(1, -1)
    tm = TOKEN_TILE
    proj_args = (h, h, g_mix, mu, w_rkv.astype(BF16), row(w0), w1.astype(BF16), w2.astype(BF16), row(a0),
                 a1.astype(BF16), a2.astype(BF16), g1.astype(BF16), g2.astype(BF16), row(k_k), row(k_a))
    before = pl.BlockSpec((SUBLANES, d), lambda i: (jnp.maximum(i * (tm // SUBLANES) - 1, 0), 0))
    dts = (BF16, F32, BF16, BF16, BF16, BF16, BF16)
    outs = pl.pallas_call(
        functools.partial(_rw_proj_body, seq=seq),
        out_shape=tuple(jax.ShapeDtypeStruct((m, d), dt) for dt in dts),
        grid=(m // tm,),
        in_specs=[_rows(tm, d), before] + [_full(a.shape) for a in proj_args[2:]],
        out_specs=tuple(_rows(tm, d) for _ in dts),
        compiler_params=_cparams("parallel"),
        name="rwkv_proj",
    )(*proj_args)
    cl = RW_CHUNK
    c2 = 2 * cl
    npair = d // LANES
    blk = pl.BlockSpec((None, cl, d), lambda b_, i: (b_, i, 0))
    vec = (row(r_k), row(gn_g), row(gn_b))
    o = pl.pallas_call(
        _rw_scan_body,
        out_shape=jax.ShapeDtypeStruct((bsz, seq, d), BF16),
        grid=(bsz, seq // cl),
        in_specs=[blk] * 7 + [pl.BlockSpec(a.shape, lambda b_, i: (0, 0)) for a in vec],
        out_specs=blk,
        scratch_shapes=[
            pltpu.VMEM((npair, c2, LANES), F32),
            pltpu.VMEM((cl, d), F32),
            pltpu.VMEM((npair, 2 * c2, LANES), BF16),
            pltpu.VMEM((npair, 2 * c2, LANES), BF16),
            pltpu.VMEM((npair, 2 * c2, LANES), BF16),
            pltpu.VMEM((npair, c2, LANES), BF16),
            pltpu.VMEM((npair, cl, LANES), F32),
            pltpu.VMEM((npair, c2, c2), BF16),
            pltpu.VMEM((npair, c2, c2), F32),
            pltpu.VMEM((npair, c2, c2), BF16),
            pltpu.VMEM((npair, c2, 2 * c2), BF16),
            pltpu.VMEM((npair, c2, LANES), BF16),
            pltpu.VMEM((npair, c2, LANES), BF16),
            pltpu.VMEM((npair, c2, LANES), F32),
            pltpu.VMEM((npair, 2 * c2, LANES), BF16),
            pltpu.VMEM((npair, c2, LANES), F32),
        ],
        compiler_params=_cparams("parallel", "arbitrary"),
        name="rwkv_scan",
    )(*[t.reshape(bsz, seq, d) for t in outs], *vec)
    return o.reshape(m, d), w_o.astype(BF16), jnp.zeros((1, d), F32)


def kernel(x, positions, norm_mix, norm_ffn, ffn_w_in, ffn_w_out, norm_final, rw_mu, rw_w_rkv, rw_w0, rw_w1, rw_w2, rw_a0, rw_a1, rw_a2, rw_g1, rw_g2, rw_k_k, rw_k_a, rw_r_k, rw_gn_g, rw_gn_b, rw_w_o, sw_w_qkv, sw_b_qkv, sw_sinks, sw_w_o, sw_b_o, sg_w_in, sg_b_in, sg_ln_g, sg_ln_b, sg_w_s, sg_b_s, sg_w_o, sg_b_o, gla_w_in, gla_w_a2, gla_b_a, gla_gn_g, gla_w_o):
    bsz, seq, d = x.shape
    depth = norm_mix.shape[0]
    h = x.reshape(bsz * seq, d)
    g_final = norm_final[None]
    w_in_all = ffn_w_in.astype(BF16)
    w_out_all = ffn_w_out.astype(BF16)
    for i in range(depth):
        t, j = i % 4, i // 4
        g_mix = norm_mix[i][None]
        pre = None
        if t == 0:
            pre = _rwkv_layer(h, g_mix, rw_mu[j], rw_w_rkv[j], rw_w0[j], rw_w1[j], rw_w2[j], rw_a0[j], rw_a1[j],
                            rw_a2[j], rw_g1[j], rw_g2[j], rw_k_k[j], rw_k_a[j], rw_r_k[j], rw_gn_g[j], rw_gn_b[j],
                            rw_w_o[j], bsz, seq)
        elif t == 1:
            pre = _swa_layer(h, positions, g_mix, sw_w_qkv[j], sw_b_qkv[j], sw_sinks[j], sw_w_o[j], sw_b_o[j], bsz, seq)
        elif t == 2:
            h = _sgu_layer(h, g_mix, sg_w_in[j], sg_b_in[j], sg_ln_g[j], sg_ln_b[j], sg_w_s[j], sg_b_s[j],
                           sg_w_o[j], sg_b_o[j])
        else:
            h = _gla_layer(h, g_mix, gla_w_in[j], gla_w_a2[j], gla_b_a[j], gla_gn_g[j], gla_w_o[j], bsz, seq)
        h = _ffn(h, norm_ffn[i][None], w_in_all, w_out_all, g_final, i, pre=pre, final=i == depth - 1)
    return h.reshape(bsz, seq, d)
```

```python
import functools

import jax
import jax.numpy as jnp
from jax import lax
from jax.experimental import pallas as pl
from jax.experimental.pallas import tpu as pltpu

F32 = jnp.float32
BF16 = jnp.bfloat16

HEAD_DIM = 64
NORM_EPS = 1e-5
RW_GN_EPS = 64e-5
SW_Q_HEADS = 16
SW_KV_HEADS = 2
SW_BLOCK = 128
ROPE_THETA = 10000.0
SG_CHUNK = 128
SG_GROUPS = 16
GLA_HEADS = 4
GLA_TAU = 16.0
GLA_CHUNK = 64
RW_CHUNK = 64
RW_STEP_CHUNKS = 4

MXU_COLS = 256
LANES = 128
SUBLANES = 8
TOKEN_TILE = 512
SGU_TILE = 1024
VMEM_LIMIT = 56 * 1024 * 1024


def _mm(a, b):
    return jnp.dot(a, b, preferred_element_type=F32)


def _mm_nt(a, b):
    return lax.dot_general(a, b, (((1,), (1,)), ((), ())), preferred_element_type=F32)


def _mm_tn(a, b):
    return lax.dot_general(a, b, (((0,), (0,)), ((), ())), preferred_element_type=F32)


def _split(x):
    hi = x.astype(BF16)
    lo = (x - hi.astype(F32)).astype(BF16)
    return hi, lo


def _mm_rhs2(a, b):
    hi, lo = _split(b)
    return _mm(a, hi) + _mm(a, lo)


def _rms(x, g):
    return x * lax.rsqrt(jnp.mean(x * x, -1, keepdims=True) + NORM_EPS) * g


def _cparams(*sem):
    return pltpu.CompilerParams(dimension_semantics=sem, vmem_limit_bytes=VMEM_LIMIT)


def _full(shape):
    nd = len(shape)
    return pl.BlockSpec(shape, lambda *_: (0,) * nd)


def _rows(tm, n):
    return pl.BlockSpec((tm, n), lambda i: (i, 0))


def _ffn_body(*refs, hidden, fc, pre, final):
    if pre:
        h_ref, a_ref, wo_ref, bo_ref, g_ref, win_ref, wout_ref, gf_ref, o_ref = refs
        x = h_ref[...] + _mm(a_ref[...], wo_ref[...]) + bo_ref[...]
    else:
        h_ref, g_ref, win_ref, wout_ref, gf_ref, o_ref = refs
        x = h_ref[...]
    xn = _rms(x, g_ref[...]).astype(BF16)
    acc = jnp.zeros_like(x)
    for c in range(hidden // fc):
        gate = _mm(xn, win_ref[:, c * fc:(c + 1) * fc])
        up = _mm(xn, win_ref[:, hidden + c * fc:hidden + (c + 1) * fc])
        act = (gate * jax.nn.sigmoid(gate) * up).astype(BF16)
        acc = acc + _mm(act, wout_ref[c * fc:(c + 1) * fc, :])
    y = x + acc
    if final:
        y = _rms(y, gf_ref[...])
    o_ref[...] = y


def _ffn(h, g, w_in, w_out, g_final, layer, *, pre=None, final=False):
    m, d = h.shape
    hidden = w_out.shape[1]
    assert hidden % MXU_COLS == 0 and w_in.shape[2] == 2 * hidden
    tm = TOKEN_TILE
    args, specs = [h], [_rows(tm, d)]
    if pre is not None:
        a, w_o, b_o = pre
        args += [a, w_o, b_o]
        specs += [_rows(tm, a.shape[1]), _full(w_o.shape), _full(b_o.shape)]
    args += [g, w_in, w_out, g_final]
    of_layer = lambda w: pl.BlockSpec((None,) + w.shape[1:], lambda i: (layer, 0, 0))
    specs += [_full(g.shape), of_layer(w_in), of_layer(w_out), _full(g_final.shape)]
    return pl.pallas_call(
        functools.partial(_ffn_body, hidden=hidden, fc=MXU_COLS, pre=pre is not None, final=final),
        out_shape=jax.ShapeDtypeStruct((m, d), F32),
        grid=(m // tm,),
        in_specs=specs,
        out_specs=_rows(tm, d),
        compiler_params=_cparams("parallel"),
        name="ffn",
    )(*args)


def _swa_qkv_body(h_ref, g_ref, w_ref, b_ref, cos_ref, sin_ref, q_ref, k_ref, v_ref, *, qd):
    xn = _rms(h_ref[...], g_ref[...]).astype(BF16)
    qkv = _mm(xn, w_ref[...]) + b_ref[...]
    cos = cos_ref[...]
    sin = sin_ref[...]
    lane = lax.broadcasted_iota(jnp.int32, cos.shape, 1)
    first_half = (lane & (HEAD_DIM - 1)) < HEAD_DIM // 2

    def rope(t):
        rot = jnp.where(first_half, pltpu.roll(t, LANES - HEAD_DIM // 2, 1), pltpu.roll(t, HEAD_DIM // 2, 1))
        return t * cos + rot * sin

    scale = HEAD_DIM ** -0.5
    for j in range(qd // LANES):
        q_ref[:, j * LANES:(j + 1) * LANES] = (rope(qkv[:, j * LANES:(j + 1) * LANES]) * scale).astype(BF16)
    kd = k_ref.shape[1]
    for j in range(kd // LANES):
        k_ref[:, j * LANES:(j + 1) * LANES] = rope(qkv[:, qd + j * LANES:qd + (j + 1) * LANES]).astype(BF16)
    v_ref[...] = qkv[:, qd + kd:].astype(BF16)


def _swa_attn_body(sink_ref, q_ref, kc_ref, kp_ref, vc_ref, vp_ref, o_ref):
    n = pl.program_id(1)
    blk = SW_BLOCK
    kcat = jnp.concatenate([kp_ref[...], kc_ref[...]], 0)
    vcat = jnp.concatenate([vp_ref[...], vc_ref[...]], 0)
    r = lax.broadcasted_iota(jnp.int32, (blk, 2 * blk), 0)
    c = lax.broadcasted_iota(jnp.int32, (blk, 2 * blk), 1)
    no_prev = jnp.where(n > 0, 0, 2 * blk)
    valid = jnp.where(c < blk, c - r - no_prev - 1, r - c + blk) >= 0
    lo = lax.broadcasted_iota(jnp.int32, (2 * blk, LANES), 1) < HEAD_DIM
    group = SW_Q_HEADS // SW_KV_HEADS
    for j in range(SW_KV_HEADS):
        kj = kcat[:, j * LANES:(j + 1) * LANES]
        vj = vcat[:, j * LANES:(j + 1) * LANES]
        zero = jnp.zeros_like(kj)
        ke = (jnp.where(lo, kj, zero), jnp.where(lo, zero, kj))
        ve = (jnp.where(lo, vj, zero), jnp.where(lo, zero, vj))
        for p in range(group // 2):
            col = (j * (group // 2) + p) * LANES
            qp = q_ref[:, col:col + LANES]
            acc = jnp.zeros((blk, LANES), F32)
            for e in range(2):
                sink = sink_ref[j * group + 2 * p + e]
                s = jnp.where(valid, _mm_nt(qp, ke[e]), -jnp.inf)
                mx = jnp.maximum(jnp.max(s, -1, keepdims=True), sink)
                pe = jnp.exp(s - mx)
                den = jnp.sum(pe, -1, keepdims=True) + jnp.exp(sink - mx)
                acc = acc + _mm(pe.astype(BF16), ve[e]) / den
            o_ref[:, col:col + LANES] = acc.astype(BF16)


def _swa_layer(h, positions, g_mix, w_qkv, b_qkv, sinks, w_o, b_o, bsz, seq):
    m, d = h.shape
    qd = SW_Q_HEADS * HEAD_DIM
    kd = SW_KV_HEADS * HEAD_DIM
    heads = [slice(base + j * HEAD_DIM, base + (j + 1) * HEAD_DIM)
             for base in (qd, qd + kd) for j in range(SW_KV_HEADS) for _ in (0, 1)]
    w = jnp.concatenate([w_qkv[:, :qd]] + [w_qkv[:, s] for s in heads], 1).astype(BF16)
    b = jnp.concatenate([b_qkv[:qd]] + [b_qkv[s] for s in heads])[None]
    inv_freq = ROPE_THETA ** (-jnp.arange(0, HEAD_DIM, 2, dtype=F32) / HEAD_DIM)
    ang = positions.astype(F32).reshape(m, 1) * inv_freq
    cos = jnp.tile(jnp.cos(ang), (1, 4))
    sin = jnp.tile(jnp.concatenate([-jnp.sin(ang), jnp.sin(ang)], -1), (1, 2))
    tm = TOKEN_TILE
    kdd = 2 * kd
    q, k, v = pl.pallas_call(
        functools.partial(_swa_qkv_body, qd=qd),
        out_shape=(jax.ShapeDtypeStruct((m, qd), BF16), jax.ShapeDtypeStruct((m, kdd), BF16),
                   jax.ShapeDtypeStruct((m, kdd), BF16)),
        grid=(m // tm,),
        in_specs=[_rows(tm, d), _full(g_mix.shape), _full(w.shape), _full(b.shape), _rows(tm, LANES), _rows(tm, LANES)],
        out_specs=(_rows(tm, qd), _rows(tm, kdd), _rows(tm, kdd)),
        compiler_params=_cparams("parallel"),
        name="swa_qkv",
    )(h, g_mix, w, b, cos, sin)
    nb = seq // SW_BLOCK
    cur = lambda n_: pl.BlockSpec((None, SW_BLOCK, n_), lambda b_, i: (b_, i, 0))
    prev = lambda n_: pl.BlockSpec((None, SW_BLOCK, n_), lambda b_, i: (b_, jnp.maximum(i - 1, 0), 0))
    k3 = k.reshape(bsz, seq, kdd)
    v3 = v.reshape(bsz, seq, kdd)
    o = pl.pallas_call(
        _swa_attn_body,
        out_shape=jax.ShapeDtypeStruct((bsz, seq, qd), BF16),
        grid=(bsz, nb),
        in_specs=[pl.BlockSpec(memory_space=pltpu.SMEM), cur(qd), cur(kdd), prev(kdd), cur(kdd), prev(kdd)],
        out_specs=cur(qd),
        compiler_params=_cparams("parallel", "parallel"),
        name="swa_attn",
    )(sinks, q.reshape(bsz, seq, qd), k3, k3, v3, v3)
    return o.reshape(m, qd), w_o.astype(BF16), b_o[None]


def _sgu_body(h_ref, g_ref, win_ref, bin_ref, lng_ref, lnb_ref, ws_ref, bs_ref, wo_ref, bo_ref, o_ref, z_ref, vn_ref, *, width):
    x = h_ref[...]
    tm = x.shape[0]
    xn = _rms(x, g_ref[...]).astype(BF16)
    gelu = lambda t: 0.5 * t * (1.0 + lax.erf(t * (0.5 ** 0.5)))
    v = gelu(_mm(xn, win_ref[:, width:]) + bin_ref[:, width:])
    mu = jnp.mean(v, -1, keepdims=True)
    vc = v - mu
    var = jnp.mean(vc * vc, -1, keepdims=True)
    vn_ref[...] = (vc * lax.rsqrt(var + NORM_EPS) * lng_ref[...] + lnb_ref[...]).astype(BF16)
    gdim = width // SG_GROUPS
    for gp in range(width // MXU_COLS):
        cp = slice(gp * MXU_COLS, (gp + 1) * MXU_COLS)
        u = gelu(_mm(xn, win_ref[:, cp]) + bin_ref[:, cp])
        for half in range(MXU_COLS // gdim):
            gi = gp * (MXU_COLS // gdim) + half
            cs = slice(gi * gdim, (gi + 1) * gdim)
            us = slice(half * gdim, (half + 1) * gdim)
            for q in range(0, tm // SG_CHUNK, 2):
                r0 = slice(q * SG_CHUNK, (q + 1) * SG_CHUNK)
                r1 = slice((q + 1) * SG_CHUNK, (q + 2) * SG_CHUNK)
                sv = _mm(ws_ref[gi], jnp.concatenate([vn_ref[r0, cs], vn_ref[r1, cs]], 1))
                z_ref[r0, cs] = (u[r0, us] * (sv[:, :gdim] + bs_ref[gi])).astype(BF16)
                z_ref[r1, cs] = (u[r1, us] * (sv[:, gdim:] + bs_ref[gi])).astype(BF16)
    o_ref[...] = x + _mm(z_ref[...], wo_ref[...]) + bo_ref[...]


def _sgu_layer(h, g_mix, w_in, b_in, ln_g, ln_b, w_s, b_s, w_o, b_o):
    m, d = h.shape
    width = w_o.shape[0]
    causal = jnp.tril(jnp.ones((SG_CHUNK, SG_CHUNK), dtype=bool))
    ws = jnp.where(causal[None], w_s, 0.0).astype(BF16)
    bs = jnp.broadcast_to(b_s[:, :, None], (SG_GROUPS, SG_CHUNK, width // SG_GROUPS))
    args = (h, g_mix, w_in.astype(BF16), b_in[None], ln_g[None], ln_b[None], ws, bs, w_o.astype(BF16), b_o[None])
    tm = SGU_TILE
    return pl.pallas_call(
        functools.partial(_sgu_body, width=width),
        out_shape=jax.ShapeDtypeStruct((m, d), F32),
        grid=(m // tm,),
        in_specs=[_rows(tm, d)] + [_full(a.shape) for a in args[1:]],
        out_specs=_rows(tm, d),
        scratch_shapes=[pltpu.VMEM((tm, width), BF16), pltpu.VMEM((tm, width), BF16)],
        compiler_params=_cparams("parallel"),
        name="sgu",
    )(*args)


def _gla_body(h_ref, g_ref, win_ref, wa1_ref, wa2_ref, ba_ref, gng_ref, wo_ref, o_ref,
              st_ref, oc_ref, qg_ref, kg_ref, ks_ref, v_ref, dl_ref, eq_ref, ek_ref, es_ref, gt_ref, *, dk, dv):
    @pl.when(pl.program_id(1) == 0)
    def _():
        st_ref[...] = jnp.zeros_like(st_ref)

    tm = h_ref.shape[0]
    hk, hv = dk // GLA_HEADS, dv // GLA_HEADS
    cl = GLA_CHUNK
    heads = [(slice(hh * hk, (hh + 1) * hk), slice(hh * hv, (hh + 1) * hv)) for hh in range(GLA_HEADS)]
    ri = lax.broadcasted_iota(jnp.int32, (cl, cl), 0)
    ci = lax.broadcasted_iota(jnp.int32, (cl, cl), 1)
    causal = ci <= ri
    tri = causal.astype(BF16)
    x = h_ref[...]
    xn = _rms(x, g_ref[...]).astype(BF16)
    chunks = [slice(c * cl, (c + 1) * cl) for c in range(tm // cl)]
    a_low = _mm(xn, wa1_ref[...])
    log_a = jax.nn.log_sigmoid(_mm(a_low.astype(BF16), wa2_ref[...]) + ba_ref[...]) / GLA_TAU
    for c, rs in enumerate(chunks):
        bcum = _mm_rhs2(tri, log_a[rs])
        b_last = bcum[cl - 1:cl]
        eq_ref[rs] = jnp.exp(bcum)
        ek_ref[rs] = jnp.exp(-bcum)
        es_ref[rs] = jnp.exp(b_last - bcum)
        dl_ref[c:c + 1] = jnp.exp(b_last)
    qg_ref[...] = (_mm(xn, win_ref[:, :dk]) * (hk ** -0.5) * eq_ref[...]).astype(BF16)
    k_all = _mm(xn, win_ref[:, dk:2 * dk])
    kg_ref[...] = (k_all * ek_ref[...]).astype(BF16)
    ks_ref[...] = (k_all * es_ref[...]).astype(BF16)
    v_ref[...] = _mm(xn, win_ref[:, 2 * dk:2 * dk + dv]).astype(BF16)
    gate = _mm(xn, win_ref[:, 2 * dk + dv:])
    gt_ref[...] = gate * jax.nn.sigmoid(gate)
    for c, rs in enumerate(chunks):
        for hh, (ks_, vs_) in enumerate(heads):
            vh = v_ref[rs, vs_]
            att = jnp.where(causal, _mm_nt(qg_ref[rs, ks_], kg_ref[rs, ks_]), 0.0)
            st = st_ref[hh]
            oc_ref[rs, vs_] = _mm(att.astype(BF16), vh) + _mm_nt(qg_ref[rs, ks_], st.astype(BF16))
            st_ref[hh] = st * dl_ref[c:c + 1, ks_] + _mm_tn(vh, ks_ref[rs, ks_])
    for hh, (ks_, vs_) in enumerate(heads):
        o = oc_ref[:, vs_]
        o = o * lax.rsqrt(jnp.mean(o * o, -1, keepdims=True) + NORM_EPS) * gng_ref[:, vs_]
        oc_ref[:, vs_] = o * gt_ref[:, vs_]
    o_ref[...] = x + _mm(oc_ref[...].astype(BF16), wo_ref[...])


def _gla_layer(h, g_mix, w_in, w_a2, b_a, gn_g, w_o, bsz, seq):
    m, d = h.shape
    dk = w_a2.shape[1]
    dv = w_o.shape[0]
    lora = w_a2.shape[0]
    main = 2 * dk + 2 * dv
    wa1 = jnp.pad(w_in[:, main:], ((0, 0), (0, LANES - lora))).astype(BF16)
    wa2 = jnp.pad(w_a2, ((0, LANES - lora), (0, 0))).astype(BF16)
    args = (h.reshape(bsz, seq, d), g_mix, w_in[:, :main].astype(BF16), wa1, wa2, b_a[None], gn_g[None], w_o.astype(BF16))
    tm = TOKEN_TILE
    hk, hv = dk // GLA_HEADS, dv // GLA_HEADS
    tile = pl.BlockSpec((None, tm, d), lambda b_, i: (b_, i, 0))
    out = pl.pallas_call(
        functools.partial(_gla_body, dk=dk, dv=dv),
        out_shape=jax.ShapeDtypeStruct((bsz, seq, d), F32),
        grid=(bsz, seq // tm),
        in_specs=[tile] + [_full(a.shape) for a in args[1:]],
        out_specs=tile,
        scratch_shapes=[
            pltpu.VMEM((GLA_HEADS, hv, hk), F32),
            pltpu.VMEM((tm, dv), F32),
            pltpu.VMEM((tm, dk), BF16),
            pltpu.VMEM((tm, dk), BF16),
            pltpu.VMEM((tm, dk), BF16),
            pltpu.VMEM((tm, dv), BF16),
            pltpu.VMEM((tm // GLA_CHUNK, dk), F32),
            pltpu.VMEM((tm, dk), F32),
            pltpu.VMEM((tm, dk), F32),
            pltpu.VMEM((tm, dk), F32),
            pltpu.VMEM((tm, dv), F32),
        ],
        compiler_params=_cparams("parallel", "arbitrary"),
        name="gla",
    )(*args)
    return out.reshape(m, d)


def _rw_proj_body(h_ref, hp_ref, g_ref, mu_ref, wrkv_ref, w0_ref, w1_ref, w2_ref, a0_ref, a1_ref, a2_ref,
                  g1_ref, g2_ref, kk_ref, ka_ref,
                  r_out, ld_out, k_out, v_out, kk_out, a_out, g_out, *, seq):
    g = g_ref[...]
    xn = _rms(h_ref[...], g)
    tm = xn.shape[0]
    starts_seq = lax.rem(pl.program_id(0) * tm, seq) == 0
    prev_row = _rms(hp_ref[...], g)[SUBLANES - 1:SUBLANES] * jnp.where(starts_seq, 0.0, 1.0)
    row = lax.broadcasted_iota(jnp.int32, xn.shape, 0)
    xx = jnp.where(row == 0, prev_row, pltpu.roll(xn, 1, 0)) - xn
    mix = lambda c: (xn + xx * mu_ref[c:c + 1]).astype(BF16)
    w_pre = w0_ref[...] + _mm(jnp.tanh(_mm(mix(1), w1_ref[...])).astype(BF16), w2_ref[...])
    ld_out[...] = -jnp.exp(-jax.nn.softplus(-w_pre) - 0.5)
    a = jax.nn.sigmoid(a0_ref[...] + _mm(_mm(mix(4), a1_ref[...]).astype(BF16), a2_ref[...]))
    a_out[...] = a.astype(BF16)
    g_out[...] = _mm(jax.nn.sigmoid(_mm(mix(5), g1_ref[...])).astype(BF16), g2_ref[...]).astype(BF16)
    r_out[...] = _mm(mix(0), wrkv_ref[0]).astype(BF16)
    v_out[...] = _mm(mix(3), wrkv_ref[2]).astype(BF16)
    k = _mm(mix(2), wrkv_ref[1])
    k_out[...] = (k * (1.0 + (a - 1.0) * ka_ref[...])).astype(BF16)
    kk_out[...] = (k * kk_ref[...]).astype(BF16)


def _stack_heads(z):
    lo = lax.broadcasted_iota(jnp.int32, z.shape, 1) < HEAD_DIM
    zero = jnp.zeros_like(z)
    return jnp.concatenate([jnp.where(lo, z, zero), jnp.where(lo, zero, z)], 0)


def _rw_scan_body(r_ref, ld_ref, k_ref, v_ref, kk_ref, a_ref, gate_ref, rk_ref, gng_ref, gnb_ref, o_ref,
                  st_ref, cum_ref, lhs_ref, rhs_ref, end_ref, vs_ref, bon_ref, pw_ref, t_ref, ak_ref, rbk_ref,
                  av_ref, g_ref, u_ref, wv_ref, yr_ref):
    @pl.when(pl.program_id(1) == 0)
    def _():
        st_ref[...] = jnp.zeros_like(st_ref)

    cl = RW_CHUNK
    nck = r_ref.shape[0] // cl
    d = r_ref.shape[1]
    npair = d // LANES
    pairs = range(npair)
    items = [(ck, p) for ck in range(nck) for p in pairs]
    c2 = 2 * cl
    bf = lambda t: t.astype(BF16)
    ri = lax.broadcasted_iota(jnp.int32, (cl, cl), 0)
    ci = lax.broadcasted_iota(jnp.int32, (cl, cl), 1)
    tri = (ci <= ri).astype(BF16)
    for ck in range(nck):
        rows = slice(ck * cl, (ck + 1) * cl)
        ld_all = ld_ref[rows]
        hi, lo = _split(ld_all)
        lo2 = (ld_all - hi.astype(F32) - lo.astype(F32)).astype(BF16)
        cum_ref[rows] = _mm(tri, hi) + (_mm(tri, lo) + _mm(tri, lo2))

    for ck, p in items:
        q = ck * npair + p
        rows = slice(ck * cl, (ck + 1) * cl)
        cs = slice(p * LANES, (p + 1) * LANES)
        cum = cum_ref[rows, cs]
        cum_end = cum_ref[(ck + 1) * cl - 1:(ck + 1) * cl, cs]
        p_inv = jnp.exp(-cum)
        to_end = jnp.exp(cum_end - cum)
        r = r_ref[rows, cs].astype(F32)
        k = k_ref[rows, cs].astype(F32)
        kk = kk_ref[rows, cs].astype(F32)
        kk_st = _stack_heads(kk)
        inv = lax.rsqrt(jnp.maximum(jnp.sum(kk_st * kk_st, -1, keepdims=True), 1e-24))
        kk = kk * jnp.where(lax.broadcasted_iota(jnp.int32, kk.shape, 1) < HEAD_DIM, inv[:cl], inv[cl:])
        b = kk * a_ref[rows, cs].astype(F32)
        vs = _stack_heads(v_ref[rows, cs])
        lhs_ref[q] = jnp.concatenate([_stack_heads(bf(kk * jnp.exp(cum - ld_ref[rows, cs]))),
                                      _stack_heads(bf(r * jnp.exp(cum)))], 0)
        rhs_ref[q] = jnp.concatenate([_stack_heads(bf(b * p_inv)), _stack_heads(bf(k * p_inv))], 0)
        end_ref[q] = jnp.concatenate([_stack_heads(bf(b * to_end)), _stack_heads(bf(k * to_end))], 0)
        vs_ref[q] = vs
        bonus = jnp.sum(_stack_heads(r * k * rk_ref[:, cs]), -1, keepdims=True) * vs.astype(F32)
        bon_ref[q] = bonus[:cl] + bonus[cl:]

    r2 = lax.broadcasted_iota(jnp.int32, (c2, c2), 0)
    q2 = lax.broadcasted_iota(jnp.int32, (c2, c2), 1)
    same = (r2 < cl) == (q2 < cl)
    strict = same & (q2 < r2)
    incl = same & (q2 <= r2)
    eye = (r2 == q2).astype(F32)
    slots = range(nck * npair)
    for q in slots:
        a_all = _mm_nt(lhs_ref[q], rhs_ref[q])
        a_ab = jnp.where(strict, a_all[:c2, :c2], 0.0)
        pw_ref[q] = bf(-a_ab)
        t_ref[q] = eye - a_ab
        ak_ref[q] = bf(jnp.where(strict, a_all[:c2, c2:], 0.0))
        rbk_ref[q, :, :c2] = bf(jnp.where(incl, a_all[c2:, :c2], 0.0))
        rbk_ref[q, :, c2:] = bf(jnp.where(incl, a_all[c2:, c2:], 0.0))

    for q in slots:
        pw = pw_ref[q]
        pw_ref[q] = bf(_mm(pw, pw))
    for _ in range(cl.bit_length() - 3):
        for q in slots:
            pw = pw_ref[q]
            t_inv = t_ref[q]
            both = _mm(pw, jnp.concatenate([pw, bf(t_inv)], 1))
            pw_ref[q] = bf(both[:, :c2])
            t_ref[q] = t_inv + both[:, c2:]
    for q in slots:
        t_inv = t_ref[q]
        t_ref[q] = t_inv + _mm(pw_ref[q], bf(t_inv))
    for q in slots:
        av_ref[q] = bf(_mm(ak_ref[q], vs_ref[q]))
    for q in slots:
        gu = _mm(bf(t_ref[q]), jnp.concatenate([lhs_ref[q, :c2], av_ref[q]], 1))
        g_ref[q] = bf(gu[:, :LANES])
        u_ref[q] = gu[:, LANES:]

    inv_n = 1.0 / HEAD_DIM
    own = lax.broadcasted_iota(jnp.int32, (c2, LANES), 1) // HEAD_DIM == lax.broadcasted_iota(jnp.int32, (c2, LANES), 0) // cl
    for ck in range(nck):
        rows = slice(ck * cl, (ck + 1) * cl)
        for p in pairs:
            q = ck * npair + p
            gr = _mm_nt(jnp.concatenate([g_ref[q], lhs_ref[q, c2:]], 0), bf(st_ref[p]))
            wv_ref[q, :c2] = bf(-(gr[:c2] + u_ref[q]))
            wv_ref[q, c2:] = vs_ref[q]
            yr_ref[q] = gr[c2:]
        for p in pairs:
            q = ck * npair + p
            cs = slice(p * LANES, (p + 1) * LANES)
            wv = wv_ref[q]
            ys = yr_ref[q] + _mm(rbk_ref[q], wv)
            st_ref[p] = st_ref[p] * jnp.exp(cum_ref[(ck + 1) * cl - 1:(ck + 1) * cl, cs]) + _mm_tn(wv, end_ref[q])
            yc = jnp.where(own, ys - jnp.sum(ys, -1, keepdims=True) * inv_n, 0.0)
            yn = yc * lax.rsqrt(jnp.sum(yc * yc, -1, keepdims=True) * inv_n + RW_GN_EPS)
            out = (yn[:cl] + yn[cl:]) * gng_ref[:, cs] + gnb_ref[:, cs] + bon_ref[q]
            o_ref[rows, cs] = (out * gate_ref[rows, cs].astype(F32)).astype(BF16)


def _rwkv_layer(h, g_mix, mu, w_rkv, w0, w1, w2, a0, a1, a2, g1, g2, k_k, k_a, r_k, gn_g, gn_b, w_o, bsz, seq):
    m, d = h.shape
    row = lambda t: t.reshape(1, -1)
    tm = TOKEN_TILE
    proj_args = (h, h, g_mix, mu, w_rkv.astype(BF16), row(w0), w1.astype(BF16), w2.astype(BF16), row(a0),
                 a1.astype(BF16), a2.astype(BF16), g1.astype(BF16), g2.astype(BF16), row(k_k), row(k_a))
    before = pl.BlockSpec((SUBLANES, d), lambda i: (jnp.maximum(i * (tm // SUBLANES) - 1, 0), 0))
    dts = (BF16, F32, BF16, BF16, BF16, BF16, BF16)
    outs = pl.pallas_call(
        functools.partial(_rw_proj_body, seq=seq),
        out_shape=tuple(jax.ShapeDtypeStruct((m, d), dt) for dt in dts),
        grid=(m // tm,),
        in_specs=[_rows(tm, d), before] + [_full(a.shape) for a in proj_args[2:]],
        out_specs=tuple(_rows(tm, d) for _ in dts),
        compiler_params=_cparams("parallel"),
        name="rwkv_proj",
    )(*proj_args)
    cl = RW_CHUNK
    c2 = 2 * cl
    npair = d // LANES
    rows = RW_STEP_CHUNKS * cl
    ns = RW_STEP_CHUNKS * npair
    blk = pl.BlockSpec((None, rows, d), lambda b_, i: (b_, i, 0))
    vec = (row(r_k), row(gn_g), row(gn_b))
    o = pl.pallas_call(
        _rw_scan_body,
        out_shape=jax.ShapeDtypeStruct((bsz, seq, d), BF16),
        grid=(bsz, seq // rows),
        in_specs=[blk] * 7 + [pl.BlockSpec(a.shape, lambda b_, i: (0, 0)) for a in vec],
        out_specs=blk,
        scratch_shapes=[
            pltpu.VMEM((npair, c2, LANES), F32),
            pltpu.VMEM((rows, d), F32),
            pltpu.VMEM((ns, 2 * c2, LANES), BF16),
            pltpu.VMEM((ns, 2 * c2, LANES), BF16),
            pltpu.VMEM((ns, 2 * c2, LANES), BF16),
            pltpu.VMEM((ns, c2, LANES), BF16),
            pltpu.VMEM((ns, cl, LANES), F32),
            pltpu.VMEM((ns, c2, c2), BF16),
            pltpu.VMEM((ns, c2, c2), F32),
            pltpu.VMEM((ns, c2, c2), BF16),
            pltpu.VMEM((ns, c2, 2 * c2), BF16),
            pltpu.VMEM((ns, c2, LANES), BF16),
            pltpu.VMEM((ns, c2, LANES), BF16),
            pltpu.VMEM((ns, c2, LANES), F32),
            pltpu.VMEM((ns, 2 * c2, LANES), BF16),
            pltpu.VMEM((ns, c2, LANES), F32),
        ],
        compiler_params=_cparams("parallel", "arbitrary"),
        name="rwkv_scan",
    )(*[t.reshape(bsz, seq, d) for t in outs], *vec)
    return o.reshape(m, d), w_o.astype(BF16), jnp.zeros((1, d), F32)


def kernel(x, positions, norm_mix, norm_ffn, ffn_w_in, ffn_w_out, norm_final, rw_mu, rw_w_rkv, rw_w0, rw_w1, rw_w2, rw_a0, rw_a1, rw_a2, rw_g1, rw_g2, rw_k_k, rw_k_a, rw_r_k, rw_gn_g, rw_gn_b, rw_w_o, sw_w_qkv, sw_b_qkv, sw_sinks, sw_w_o, sw_b_o, sg_w_in, sg_b_in, sg_ln_g, sg_ln_b, sg_w_s, sg_b_s, sg_w_o, sg_b_o, gla_w_in, gla_w_a2, gla_b_a, gla_gn_g, gla_w_o):
    bsz, seq, d = x.shape
    depth = norm_mix.shape[0]
    h = x.reshape(bsz * seq, d)
    g_final = norm_final[None]
    w_in_all = ffn_w_in.astype(BF16)
    w_out_all = ffn_w_out.astype(BF16)
    for i in range(depth):
        t, j = i % 4, i // 4
        g_mix = norm_mix[i][None]
        pre = None
        if t == 0:
            pre = _rwkv_layer(h, g_mix, rw_mu[j], rw_w_rkv[j], rw_w0[j], rw_w1[j], rw_w2[j], rw_a0[j], rw_a1[j],
                            rw_a2[j], rw_g1[j], rw_g2[j], rw_k_k[j], rw_k_a[j], rw_r_k[j], rw_gn_g[j], rw_gn_b[j],
                            rw_w_o[j], bsz, seq)
        elif t == 1:
            pre = _swa_layer(h, positions, g_mix, sw_w_qkv[j], sw_b_qkv[j], sw_sinks[j], sw_w_o[j], sw_b_o[j], bsz, seq)
        elif t == 2:
            h = _sgu_layer(h, g_mix, sg_w_in[j], sg_b_in[j], sg_ln_g[j], sg_ln_b[j], sg_w_s[j], sg_b_s[j],
                           sg_w_o[j], sg_b_o[j])
        else:
            h = _gla_layer(h, g_mix, gla_w_in[j], gla_w_a2[j], gla_b_a[j], gla_gn_g[j], gla_w_o[j], bsz, seq)
        h = _ffn(h, norm_ffn[i][None], w_in_all, w_out_all, g_final, i, pre=pre, final=i == depth - 1)
    return h.reshape(bsz, seq, d)
```

```python
import functools

import jax
import jax.numpy as jnp
from jax import lax
from jax.experimental import pallas as pl
from jax.experimental.pallas import tpu as pltpu

F32 = jnp.float32
BF16 = jnp.bfloat16

HEAD_DIM = 64
NORM_EPS = 1e-5
RW_GN_EPS = 64e-5
SW_Q_HEADS = 16
SW_KV_HEADS = 2
SW_BLOCK = 128
SW_STEP_BLOCKS = 8
ROPE_THETA = 10000.0
SG_CHUNK = 128
SG_GROUPS = 16
GLA_HEADS = 4
GLA_TAU = 16.0
GLA_CHUNK = 64
RW_CHUNK = 64
RW_STEP_CHUNKS = 4

MXU_COLS = 256
LANES = 128
SUBLANES = 8
TOKEN_TILE = 512
SGU_TILE = 1024
VMEM_LIMIT = 56 * 1024 * 1024


def _mm(a, b):
    return jnp.dot(a, b, preferred_element_type=F32)


def _mm_nt(a, b):
    return lax.dot_general(a, b, (((1,), (1,)), ((), ())), preferred_element_type=F32)


def _mm_tn(a, b):
    return lax.dot_general(a, b, (((0,), (0,)), ((), ())), preferred_element_type=F32)


def _split(x):
    hi = x.astype(BF16)
    lo = (x - hi.astype(F32)).astype(BF16)
    return hi, lo


def _mm_rhs2(a, b):
    hi, lo = _split(b)
    return _mm(a, hi) + _mm(a, lo)


def _rms(x, g):
    return x * lax.rsqrt(jnp.mean(x * x, -1, keepdims=True) + NORM_EPS) * g


def _cparams(*sem):
    return pltpu.CompilerParams(dimension_semantics=sem, vmem_limit_bytes=VMEM_LIMIT)


def _full(shape):
    nd = len(shape)
    return pl.BlockSpec(shape, lambda *_: (0,) * nd)


def _rows(tm, n):
    return pl.BlockSpec((tm, n), lambda i: (i, 0))


def _ffn_body(*refs, hidden, fc, pre, final):
    if pre:
        h_ref, a_ref, wo_ref, bo_ref, g_ref, win_ref, wout_ref, gf_ref, o_ref = refs
        x = h_ref[...] + _mm(a_ref[...], wo_ref[...]) + bo_ref[...]
    else:
        h_ref, g_ref, win_ref, wout_ref, gf_ref, o_ref = refs
        x = h_ref[...]
    xn = _rms(x, g_ref[...]).astype(BF16)
    acc = jnp.zeros_like(x)
    for c in range(hidden // fc):
        gate = _mm(xn, win_ref[:, c * fc:(c + 1) * fc].astype(BF16))
        up = _mm(xn, win_ref[:, hidden + c * fc:hidden + (c + 1) * fc].astype(BF16))
        act = (gate * jax.nn.sigmoid(gate) * up).astype(BF16)
        acc = acc + _mm(act, wout_ref[c * fc:(c + 1) * fc, :].astype(BF16))
    y = x + acc
    if final:
        y = _rms(y, gf_ref[...])
    o_ref[...] = y


def _ffn(h, g, w_in, w_out, g_final, layer, *, pre=None, final=False):
    m, d = h.shape
    hidden = w_out.shape[1]
    assert hidden % MXU_COLS == 0 and w_in.shape[2] == 2 * hidden
    tm = TOKEN_TILE
    args, specs = [h], [_rows(tm, d)]
    if pre is not None:
        a, w_o, b_o = pre
        args += [a, w_o, b_o]
        specs += [_rows(tm, a.shape[1]), _full(w_o.shape), _full(b_o.shape)]
    args += [g, w_in, w_out, g_final]
    of_layer = lambda w: pl.BlockSpec((None,) + w.shape[1:], lambda i: (layer, 0, 0), pipeline_mode=pl.Buffered(1))
    specs += [_full(g.shape), of_layer(w_in), of_layer(w_out), _full(g_final.shape)]
    return pl.pallas_call(
        functools.partial(_ffn_body, hidden=hidden, fc=MXU_COLS, pre=pre is not None, final=final),
        out_shape=jax.ShapeDtypeStruct((m, d), F32),
        grid=(m // tm,),
        in_specs=specs,
        out_specs=_rows(tm, d),
        compiler_params=_cparams("parallel"),
        name="ffn",
    )(*args)


def _swa_qkv_body(h_ref, g_ref, w_ref, b_ref, cos_ref, sin_ref, q_ref, k_ref, v_ref, *, qd):
    xn = _rms(h_ref[...], g_ref[...]).astype(BF16)
    qkv = _mm(xn, w_ref[...]) + b_ref[...]
    cos = cos_ref[...]
    sin = sin_ref[...]
    lane = lax.broadcasted_iota(jnp.int32, cos.shape, 1)
    first_half = (lane & (HEAD_DIM - 1)) < HEAD_DIM // 2

    def rope(t):
        rot = jnp.where(first_half, pltpu.roll(t, LANES - HEAD_DIM // 2, 1), pltpu.roll(t, HEAD_DIM // 2, 1))
        return t * cos + rot * sin

    scale = HEAD_DIM ** -0.5
    for j in range(qd // LANES):
        q_ref[:, j * LANES:(j + 1) * LANES] = (rope(qkv[:, j * LANES:(j + 1) * LANES]) * scale).astype(BF16)
    kd = k_ref.shape[1]
    for j in range(kd // LANES):
        k_ref[:, j * LANES:(j + 1) * LANES] = rope(qkv[:, qd + j * LANES:qd + (j + 1) * LANES]).astype(BF16)
    v_ref[...] = qkv[:, qd + kd:].astype(BF16)


def _swa_attn_body(sink_ref, q_ref, kc_ref, kp_ref, vc_ref, vp_ref, o_ref, s_ref, p_ref, rd_ref):
    n = pl.program_id(1)
    blk = SW_BLOCK
    r = lax.broadcasted_iota(jnp.int32, (blk, blk), 0)
    c = lax.broadcasted_iota(jnp.int32, (blk, blk), 1)
    from_prev = c > r
    no_prev = jnp.where(n > 0, 0.0, -jnp.inf)
    lo = lax.broadcasted_iota(jnp.int32, (blk, LANES), 1) < HEAD_DIM
    group = SW_Q_HEADS // SW_KV_HEADS
    zero = jnp.zeros((blk, LANES), BF16)
    halves = lambda t: (jnp.where(lo, t, zero), jnp.where(lo, zero, t))
    kv_cols = [slice(j * LANES, (j + 1) * LANES) for j in range(SW_KV_HEADS)]
    heads = [(j, p, e) for j in range(SW_KV_HEADS) for p in range(group // 2) for e in range(2)]
    q_cols = lambda j, p: slice((j * (group // 2) + p) * LANES, (j * (group // 2) + p + 1) * LANES)
    nblk = q_ref.shape[0] // blk
    rows = [slice(i * blk, (i + 1) * blk) for i in range(nblk)]
    keys = [[halves(kp_ref[:, cj]) for cj in kv_cols]] + [[halves(kc_ref[rs, cj]) for cj in kv_cols] for rs in rows]
    for i, rs in enumerate(rows):
        for h_, (j, p, e) in enumerate(heads):
            qp = q_ref[rs, q_cols(j, p)]
            s_prev = _mm_nt(qp, keys[i][j][e])
            if i == 0:
                s_prev = s_prev + no_prev
            s_ref[i * len(heads) + h_] = jnp.where(from_prev, s_prev, _mm_nt(qp, keys[i + 1][j][e]))
    for i in range(nblk):
        for h_, (j, p, e) in enumerate(heads):
            it = i * len(heads) + h_
            sink = sink_ref[j * group + 2 * p + e]
            s = s_ref[it]
            mx = jnp.maximum(jnp.max(s, -1, keepdims=True), sink)
            pe = jnp.exp(s - mx)
            rd_ref[it] = jnp.broadcast_to(1.0 / (jnp.sum(pe, -1, keepdims=True) + jnp.exp(sink - mx)), (blk, LANES))
            pb = pe.astype(BF16)
            p_ref[it, :, :blk] = jnp.where(from_prev, pb, zero)
            p_ref[it, :, blk:] = jnp.where(from_prev, zero, pb)
    vals = [[halves(vp_ref[:, cj]) for cj in kv_cols]] + [[halves(vc_ref[rs, cj]) for cj in kv_cols] for rs in rows]
    for i, rs in enumerate(rows):
        for j in range(SW_KV_HEADS):
            vcat = [jnp.concatenate([vals[i][j][e], vals[i + 1][j][e]], 0) for e in range(2)]
            for p in range(group // 2):
                it = i * len(heads) + heads.index((j, p, 0))
                o_ref[rs, q_cols(j, p)] = (_mm(p_ref[it], vcat[0]) * rd_ref[it]
                                           + _mm(p_ref[it + 1], vcat[1]) * rd_ref[it + 1]).astype(BF16)


def _swa_layer(h, positions, g_mix, w_qkv, b_qkv, sinks, w_o, b_o, bsz, seq):
    m, d = h.shape
    qd = SW_Q_HEADS * HEAD_DIM
    kd = SW_KV_HEADS * HEAD_DIM
    heads = [slice(base + j * HEAD_DIM, base + (j + 1) * HEAD_DIM)
             for base in (qd, qd + kd) for j in range(SW_KV_HEADS) for _ in (0, 1)]
    w = jnp.concatenate([w_qkv[:, :qd]] + [w_qkv[:, s] for s in heads], 1).astype(BF16)
    b = jnp.concatenate([b_qkv[:qd]] + [b_qkv[s] for s in heads])[None]
    inv_freq = ROPE_THETA ** (-jnp.arange(0, HEAD_DIM, 2, dtype=F32) / HEAD_DIM)
    ang = positions.astype(F32).reshape(m, 1) * inv_freq
    cos = jnp.tile(jnp.cos(ang), (1, 4))
    sin = jnp.tile(jnp.concatenate([-jnp.sin(ang), jnp.sin(ang)], -1), (1, 2))
    tm = TOKEN_TILE
    kdd = 2 * kd
    q, k, v = pl.pallas_call(
        functools.partial(_swa_qkv_body, qd=qd),
        out_shape=(jax.ShapeDtypeStruct((m, qd), BF16), jax.ShapeDtypeStruct((m, kdd), BF16),
                   jax.ShapeDtypeStruct((m, kdd), BF16)),
        grid=(m // tm,),
        in_specs=[_rows(tm, d), _full(g_mix.shape), _full(w.shape), _full(b.shape), _rows(tm, LANES), _rows(tm, LANES)],
        out_specs=(_rows(tm, qd), _rows(tm, kdd), _rows(tm, kdd)),
        compiler_params=_cparams("parallel"),
        name="swa_qkv",
    )(h, g_mix, w, b, cos, sin)
    sb = SW_STEP_BLOCKS
    cur = lambda n_: pl.BlockSpec((None, sb * SW_BLOCK, n_), lambda b_, i: (b_, i, 0))
    prev = lambda n_: pl.BlockSpec((None, SW_BLOCK, n_), lambda b_, i: (b_, jnp.maximum(i * sb - 1, 0), 0))
    k3 = k.reshape(bsz, seq, kdd)
    v3 = v.reshape(bsz, seq, kdd)
    items = sb * SW_Q_HEADS
    o = pl.pallas_call(
        _swa_attn_body,
        out_shape=jax.ShapeDtypeStruct((bsz, seq, qd), BF16),
        grid=(bsz, seq // (sb * SW_BLOCK)),
        in_specs=[pl.BlockSpec(memory_space=pltpu.SMEM), cur(qd), cur(kdd), prev(kdd), cur(kdd), prev(kdd)],
        out_specs=cur(qd),
        scratch_shapes=[
            pltpu.VMEM((items, SW_BLOCK, SW_BLOCK), F32),
            pltpu.VMEM((items, SW_BLOCK, 2 * SW_BLOCK), BF16),
            pltpu.VMEM((items, SW_BLOCK, LANES), F32),
        ],
        compiler_params=_cparams("parallel", "parallel"),
        name="swa_attn",
    )(sinks, q.reshape(bsz, seq, qd), k3, k3, v3, v3)
    return o.reshape(m, qd), w_o.astype(BF16), b_o[None]


def _sgu_body(h_ref, g_ref, win_ref, bin_ref, lng_ref, lnb_ref, ws_ref, bs_ref, wo_ref, bo_ref, o_ref, z_ref, vn_ref, *, width):
    x = h_ref[...]
    tm = x.shape[0]
    xn = _rms(x, g_ref[...]).astype(BF16)
    gelu = lambda t: 0.5 * t * (1.0 + lax.erf(t * (0.5 ** 0.5)))
    v = gelu(_mm(xn, win_ref[:, width:]) + bin_ref[:, width:])
    mu = jnp.mean(v, -1, keepdims=True)
    vc = v - mu
    var = jnp.mean(vc * vc, -1, keepdims=True)
    vn_ref[...] = (vc * lax.rsqrt(var + NORM_EPS) * lng_ref[...] + lnb_ref[...]).astype(BF16)
    gdim = width // SG_GROUPS
    for gp in range(width // MXU_COLS):
        cp = slice(gp * MXU_COLS, (gp + 1) * MXU_COLS)
        u = gelu(_mm(xn, win_ref[:, cp]) + bin_ref[:, cp])
        for half in range(MXU_COLS // gdim):
            gi = gp * (MXU_COLS // gdim) + half
            cs = slice(gi * gdim, (gi + 1) * gdim)
            us = slice(half * gdim, (half + 1) * gdim)
            for q in range(0, tm // SG_CHUNK, 2):
                r0 = slice(q * SG_CHUNK, (q + 1) * SG_CHUNK)
                r1 = slice((q + 1) * SG_CHUNK, (q + 2) * SG_CHUNK)
                sv = _mm(ws_ref[gi], jnp.concatenate([vn_ref[r0, cs], vn_ref[r1, cs]], 1))
                z_ref[r0, cs] = (u[r0, us] * (sv[:, :gdim] + bs_ref[gi])).astype(BF16)
                z_ref[r1, cs] = (u[r1, us] * (sv[:, gdim:] + bs_ref[gi])).astype(BF16)
    o_ref[...] = x + _mm(z_ref[...], wo_ref[...]) + bo_ref[...]


def _sgu_layer(h, g_mix, w_in, b_in, ln_g, ln_b, w_s, b_s, w_o, b_o):
    m, d = h.shape
    width = w_o.shape[0]
    causal = jnp.tril(jnp.ones((SG_CHUNK, SG_CHUNK), dtype=bool))
    ws = jnp.where(causal[None], w_s, 0.0).astype(BF16)
    bs = jnp.broadcast_to(b_s[:, :, None], (SG_GROUPS, SG_CHUNK, width // SG_GROUPS))
    args = (h, g_mix, w_in.astype(BF16), b_in[None], ln_g[None], ln_b[None], ws, bs, w_o.astype(BF16), b_o[None])
    tm = SGU_TILE
    return pl.pallas_call(
        functools.partial(_sgu_body, width=width),
        out_shape=jax.ShapeDtypeStruct((m, d), F32),
        grid=(m // tm,),
        in_specs=[_rows(tm, d)] + [_full(a.shape) for a in args[1:]],
        out_specs=_rows(tm, d),
        scratch_shapes=[pltpu.VMEM((tm, width), BF16), pltpu.VMEM((tm, width), BF16)],
        compiler_params=_cparams("parallel"),
        name="sgu",
    )(*args)


def _gla_body(h_ref, g_ref, win_ref, wa1_ref, wa2_ref, ba_ref, gng_ref, wo_ref, o_ref,
              st_ref, oc_ref, qg_ref, kg_ref, ks_ref, v_ref, dl_ref, eq_ref, ek_ref, es_ref, gt_ref, *, dk, dv):
    @pl.when(pl.program_id(1) == 0)
    def _():
        st_ref[...] = jnp.zeros_like(st_ref)

    tm = h_ref.shape[0]
    hk, hv = dk // GLA_HEADS, dv // GLA_HEADS
    cl = GLA_CHUNK
    heads = [(slice(hh * hk, (hh + 1) * hk), slice(hh * hv, (hh + 1) * hv)) for hh in range(GLA_HEADS)]
    ri = lax.broadcasted_iota(jnp.int32, (cl, cl), 0)
    ci = lax.broadcasted_iota(jnp.int32, (cl, cl), 1)
    causal = ci <= ri
    tri = causal.astype(BF16)
    x = h_ref[...]
    xn = _rms(x, g_ref[...]).astype(BF16)
    chunks = [slice(c * cl, (c + 1) * cl) for c in range(tm // cl)]
    a_low = _mm(xn, wa1_ref[...])
    log_a = jax.nn.log_sigmoid(_mm(a_low.astype(BF16), wa2_ref[...]) + ba_ref[...]) / GLA_TAU
    for c, rs in enumerate(chunks):
        bcum = _mm_rhs2(tri, log_a[rs])
        b_last = bcum[cl - 1:cl]
        eq_ref[rs] = jnp.exp(bcum)
        ek_ref[rs] = jnp.exp(-bcum)
        es_ref[rs] = jnp.exp(b_last - bcum)
        dl_ref[c:c + 1] = jnp.exp(b_last)
    qg_ref[...] = (_mm(xn, win_ref[:, :dk]) * (hk ** -0.5) * eq_ref[...]).astype(BF16)
    k_all = _mm(xn, win_ref[:, dk:2 * dk])
    kg_ref[...] = (k_all * ek_ref[...]).astype(BF16)
    ks_ref[...] = (k_all * es_ref[...]).astype(BF16)
    v_ref[...] = _mm(xn, win_ref[:, 2 * dk:2 * dk + dv]).astype(BF16)
    gate = _mm(xn, win_ref[:, 2 * dk + dv:])
    gt_ref[...] = gate * jax.nn.sigmoid(gate)
    for c, rs in enumerate(chunks):
        for hh, (ks_, vs_) in enumerate(heads):
            vh = v_ref[rs, vs_]
            att = jnp.where(causal, _mm_nt(qg_ref[rs, ks_], kg_ref[rs, ks_]), 0.0)
            st = st_ref[hh]
            oc_ref[rs, vs_] = _mm(att.astype(BF16), vh) + _mm_nt(qg_ref[rs, ks_], st.astype(BF16))
            st_ref[hh] = st * dl_ref[c:c + 1, ks_] + _mm_tn(vh, ks_ref[rs, ks_])
    for hh, (ks_, vs_) in enumerate(heads):
        o = oc_ref[:, vs_]
        o = o * lax.rsqrt(jnp.mean(o * o, -1, keepdims=True) + NORM_EPS) * gng_ref[:, vs_]
        oc_ref[:, vs_] = o * gt_ref[:, vs_]
    o_ref[...] = x + _mm(oc_ref[...].astype(BF16), wo_ref[...])


def _gla_layer(h, g_mix, w_in, w_a2, b_a, gn_g, w_o, bsz, seq):
    m, d = h.shape
    dk = w_a2.shape[1]
    dv = w_o.shape[0]
    lora = w_a2.shape[0]
    main = 2 * dk + 2 * dv
    wa1 = jnp.pad(w_in[:, main:], ((0, 0), (0, LANES - lora))).astype(BF16)
    wa2 = jnp.pad(w_a2, ((0, LANES - lora), (0, 0))).astype(BF16)
    args = (h.reshape(bsz, seq, d), g_mix, w_in[:, :main].astype(BF16), wa1, wa2, b_a[None], gn_g[None], w_o.astype(BF16))
    tm = TOKEN_TILE
    hk, hv = dk // GLA_HEADS, dv // GLA_HEADS
    tile = pl.BlockSpec((None, tm, d), lambda b_, i: (b_, i, 0))
    out = pl.pallas_call(
        functools.partial(_gla_body, dk=dk, dv=dv),
        out_shape=jax.ShapeDtypeStruct((bsz, seq, d), F32),
        grid=(bsz, seq // tm),
        in_specs=[tile] + [_full(a.shape) for a in args[1:]],
        out_specs=tile,
        scratch_shapes=[
            pltpu.VMEM((GLA_HEADS, hv, hk), F32),
            pltpu.VMEM((tm, dv), F32),
            pltpu.VMEM((tm, dk), BF16),
            pltpu.VMEM((tm, dk), BF16),
            pltpu.VMEM((tm, dk), BF16),
            pltpu.VMEM((tm, dv), BF16),
            pltpu.VMEM((tm // GLA_CHUNK, dk), F32),
            pltpu.VMEM((tm, dk), F32),
            pltpu.VMEM((tm, dk), F32),
            pltpu.VMEM((tm, dk), F32),
            pltpu.VMEM((tm, dv), F32),
        ],
        compiler_params=_cparams("parallel", "arbitrary"),
        name="gla",
    )(*args)
    return out.reshape(m, d)


def _rw_proj_body(h_ref, hp_ref, g_ref, mu_ref, wrkv_ref, w0_ref, w1_ref, w2_ref, a0_ref, a1_ref, a2_ref,
                  g1_ref, g2_ref, kk_ref, ka_ref,
                  r_out, ld_out, k_out, v_out, kk_out, a_out, g_out, *, seq):
    g = g_ref[...]
    xn = _rms(h_ref[...], g)
    tm = xn.shape[0]
    starts_seq = lax.rem(pl.program_id(0) * tm, seq) == 0
    prev_row = _rms(hp_ref[...], g)[SUBLANES - 1:SUBLANES] * jnp.where(starts_seq, 0.0, 1.0)
    row = lax.broadcasted_iota(jnp.int32, xn.shape, 0)
    xx = jnp.where(row == 0, prev_row, pltpu.roll(xn, 1, 0)) - xn
    mix = lambda c: (xn + xx * mu_ref[c:c + 1]).astype(BF16)
    w_pre = w0_ref[...] + _mm(jnp.tanh(_mm(mix(1), w1_ref[...])).astype(BF16), w2_ref[...])
    ld_out[...] = -jnp.exp(-jax.nn.softplus(-w_pre) - 0.5)
    a = jax.nn.sigmoid(a0_ref[...] + _mm(_mm(mix(4), a1_ref[...]).astype(BF16), a2_ref[...]))
    a_out[...] = a.astype(BF16)
    g_out[...] = _mm(jax.nn.sigmoid(_mm(mix(5), g1_ref[...])).astype(BF16), g2_ref[...]).astype(BF16)
    r_out[...] = _mm(mix(0), wrkv_ref[0]).astype(BF16)
    v_out[...] = _mm(mix(3), wrkv_ref[2]).astype(BF16)
    k = _mm(mix(2), wrkv_ref[1])
    k_out[...] = (k * (1.0 + (a - 1.0) * ka_ref[...])).astype(BF16)
    kk_out[...] = (k * kk_ref[...]).astype(BF16)


def _stack_heads(z):
    lo = lax.broadcasted_iota(jnp.int32, z.shape, 1) < HEAD_DIM
    zero = jnp.zeros_like(z)
    return jnp.concatenate([jnp.where(lo, z, zero), jnp.where(lo, zero, z)], 0)


def _rw_scan_body(r_ref, ld_ref, k_ref, v_ref, kk_ref, a_ref, gate_ref, rk_ref, gng_ref, gnb_ref, o_ref,
                  st_ref, cum_ref, lhs_ref, rhs_ref, end_ref, vs_ref, bon_ref, pw_ref, t_ref, ak_ref, rbk_ref,
                  av_ref, g_ref, u_ref, wv_ref, yr_ref):
    @pl.when(pl.program_id(1) == 0)
    def _():
        st_ref[...] = jnp.zeros_like(st_ref)

    cl = RW_CHUNK
    nck = r_ref.shape[0] // cl
    d = r_ref.shape[1]
    npair = d // LANES
    pairs = range(npair)
    items = [(ck, p) for ck in range(nck) for p in pairs]
    c2 = 2 * cl
    bf = lambda t: t.astype(BF16)
    ri = lax.broadcasted_iota(jnp.int32, (cl, cl), 0)
    ci = lax.broadcasted_iota(jnp.int32, (cl, cl), 1)
    tri = (ci <= ri).astype(BF16)
    for ck in range(nck):
        rows = slice(ck * cl, (ck + 1) * cl)
        ld_all = ld_ref[rows]
        hi, lo = _split(ld_all)
        lo2 = (ld_all - hi.astype(F32) - lo.astype(F32)).astype(BF16)
        cum_ref[rows] = _mm(tri, hi) + (_mm(tri, lo) + _mm(tri, lo2))

    for ck, p in items:
        q = ck * npair + p
        rows = slice(ck * cl, (ck + 1) * cl)
        cs = slice(p * LANES, (p + 1) * LANES)
        cum = cum_ref[rows, cs]
        cum_end = cum_ref[(ck + 1) * cl - 1:(ck + 1) * cl, cs]
        p_inv = jnp.exp(-cum)
        to_end = jnp.exp(cum_end - cum)
        r = r_ref[rows, cs].astype(F32)
        k = k_ref[rows, cs].astype(F32)
        kk = kk_ref[rows, cs].astype(F32)
        kk_st = _stack_heads(kk)
        inv = lax.rsqrt(jnp.maximum(jnp.sum(kk_st * kk_st, -1, keepdims=True), 1e-24))
        kk = kk * jnp.where(lax.broadcasted_iota(jnp.int32, kk.shape, 1) < HEAD_DIM, inv[:cl], inv[cl:])
        b = kk * a_ref[rows, cs].astype(F32)
        vs = _stack_heads(v_ref[rows, cs])
        lhs_ref[q] = jnp.concatenate([_stack_heads(bf(kk * jnp.exp(cum - ld_ref[rows, cs]))),
                                      _stack_heads(bf(r * jnp.exp(cum)))], 0)
        rhs_ref[q] = jnp.concatenate([_stack_heads(bf(b * p_inv)), _stack_heads(bf(k * p_inv))], 0)
        end_ref[q] = jnp.concatenate([_stack_heads(bf(b * to_end)), _stack_heads(bf(k * to_end))], 0)
        vs_ref[q] = vs
        bonus = jnp.sum(_stack_heads(r * k * rk_ref[:, cs]), -1, keepdims=True) * vs.astype(F32)
        bon_ref[q] = bonus[:cl] + bonus[cl:]

    r2 = lax.broadcasted_iota(jnp.int32, (c2, c2), 0)
    q2 = lax.broadcasted_iota(jnp.int32, (c2, c2), 1)
    same = (r2 < cl) == (q2 < cl)
    strict = same & (q2 < r2)
    incl = same & (q2 <= r2)
    eye = (r2 == q2).astype(F32)
    slots = range(nck * npair)
    for q in slots:
        a_all = _mm_nt(lhs_ref[q], rhs_ref[q])
        a_ab = jnp.where(strict, a_all[:c2, :c2], 0.0)
        pw_ref[q] = bf(-a_ab)
        t_ref[q] = eye - a_ab
        ak_ref[q] = bf(jnp.where(strict, a_all[:c2, c2:], 0.0))
        rbk_ref[q, :, :c2] = bf(jnp.where(incl, a_all[c2:, :c2], 0.0))
        rbk_ref[q, :, c2:] = bf(jnp.where(incl, a_all[c2:, c2:], 0.0))

    for q in slots:
        pw = pw_ref[q]
        pw_ref[q] = bf(_mm(pw, pw))
    for _ in range(cl.bit_length() - 3):
        for q in slots:
            pw = pw_ref[q]
            t_inv = t_ref[q]
            both = _mm(pw, jnp.concatenate([pw, bf(t_inv)], 1))
            pw_ref[q] = bf(both[:, :c2])
            t_ref[q] = t_inv + both[:, c2:]
    for q in slots:
        t_inv = t_ref[q]
        t_ref[q] = t_inv + _mm(pw_ref[q], bf(t_inv))
    for q in slots:
        av_ref[q] = bf(_mm(ak_ref[q], vs_ref[q]))
    for q in slots:
        gu = _mm(bf(t_ref[q]), jnp.concatenate([lhs_ref[q, :c2], av_ref[q]], 1))
        g_ref[q] = bf(gu[:, :LANES])
        u_ref[q] = gu[:, LANES:]

    inv_n = 1.0 / HEAD_DIM
    own = lax.broadcasted_iota(jnp.int32, (c2, LANES), 1) // HEAD_DIM == lax.broadcasted_iota(jnp.int32, (c2, LANES), 0) // cl
    for ck in range(nck):
        rows = slice(ck * cl, (ck + 1) * cl)
        for p in pairs:
            q = ck * npair + p
            gr = _mm_nt(jnp.concatenate([g_ref[q], lhs_ref[q, c2:]], 0), bf(st_ref[p]))
            wv_ref[q, :c2] = bf(-(gr[:c2] + u_ref[q]))
            wv_ref[q, c2:] = vs_ref[q]
            yr_ref[q] = gr[c2:]
        for p in pairs:
            q = ck * npair + p
            cs = slice(p * LANES, (p + 1) * LANES)
            wv = wv_ref[q]
            ys = yr_ref[q] + _mm(rbk_ref[q], wv)
            st_ref[p] = st_ref[p] * jnp.exp(cum_ref[(ck + 1) * cl - 1:(ck + 1) * cl, cs]) + _mm_tn(wv, end_ref[q])
            yc = jnp.where(own, ys - jnp.sum(ys, -1, keepdims=True) * inv_n, 0.0)
            yn = yc * lax.rsqrt(jnp.sum(yc * yc, -1, keepdims=True) * inv_n + RW_GN_EPS)
            out = (yn[:cl] + yn[cl:]) * gng_ref[:, cs] + gnb_ref[:, cs] + bon_ref[q]
            o_ref[rows, cs] = (out * gate_ref[rows, cs].astype(F32)).astype(BF16)


def _rwkv_layer(h, g_mix, mu, w_rkv, w0, w1, w2, a0, a1, a2, g1, g2, k_k, k_a, r_k, gn_g, gn_b, w_o, bsz, seq):
    m, d = h.shape
    row = lambda t: t.reshape(1, -1)
    tm = TOKEN_TILE
    proj_args = (h, h, g_mix, mu, w_rkv.astype(BF16), row(w0), w1.astype(BF16), w2.astype(BF16), row(a0),
                 a1.astype(BF16), a2.astype(BF16), g1.astype(BF16), g2.astype(BF16), row(k_k), row(k_a))
    before = pl.BlockSpec((SUBLANES, d), lambda i: (jnp.maximum(i * (tm // SUBLANES) - 1, 0), 0))
    dts = (BF16, F32, BF16, BF16, BF16, BF16, BF16)
    outs = pl.pallas_call(
        functools.partial(_rw_proj_body, seq=seq),
        out_shape=tuple(jax.ShapeDtypeStruct((m, d), dt) for dt in dts),
        grid=(m // tm,),
        in_specs=[_rows(tm, d), before] + [_full(a.shape) for a in proj_args[2:]],
        out_specs=tuple(_rows(tm, d) for _ in dts),
        compiler_params=_cparams("parallel"),
        name="rwkv_proj",
    )(*proj_args)
    cl = RW_CHUNK
    c2 = 2 * cl
    npair = d // LANES
    rows = RW_STEP_CHUNKS * cl
    ns = RW_STEP_CHUNKS * npair
    blk = pl.BlockSpec((None, rows, d), lambda b_, i: (b_, i, 0))
    vec = (row(r_k), row(gn_g), row(gn_b))
    o = pl.pallas_call(
        _rw_scan_body,
        out_shape=jax.ShapeDtypeStruct((bsz, seq, d), BF16),
        grid=(bsz, seq // rows),
        in_specs=[blk] * 7 + [pl.BlockSpec(a.shape, lambda b_, i: (0, 0)) for a in vec],
        out_specs=blk,
        scratch_shapes=[
            pltpu.VMEM((npair, c2, LANES), F32),
            pltpu.VMEM((rows, d), F32),
            pltpu.VMEM((ns, 2 * c2, LANES), BF16),
            pltpu.VMEM((ns, 2 * c2, LANES), BF16),
            pltpu.VMEM((ns, 2 * c2, LANES), BF16),
            pltpu.VMEM((ns, c2, LANES), BF16),
            pltpu.VMEM((ns, cl, LANES), F32),
            pltpu.VMEM((ns, c2, c2), BF16),
            pltpu.VMEM((ns, c2, c2), F32),
            pltpu.VMEM((ns, c2, c2), BF16),
            pltpu.VMEM((ns, c2, 2 * c2), BF16),
            pltpu.VMEM((ns, c2, LANES), BF16),
            pltpu.VMEM((ns, c2, LANES), BF16),
            pltpu.VMEM((ns, c2, LANES), F32),
            pltpu.VMEM((ns, 2 * c2, LANES), BF16),
            pltpu.VMEM((ns, c2, LANES), F32),
        ],
        compiler_params=_cparams("parallel", "arbitrary"),
        name="rwkv_scan",
    )(*[t.reshape(bsz, seq, d) for t in outs], *vec)
    return o.reshape(m, d), w_o.astype(BF16), jnp.zeros((1, d), F32)


def kernel(x, positions, norm_mix, norm_ffn, ffn_w_in, ffn_w_out, norm_final, rw_mu, rw_w_rkv, rw_w0, rw_w1, rw_w2, rw_a0, rw_a1, rw_a2, rw_g1, rw_g2, rw_k_k, rw_k_a, rw_r_k, rw_gn_g, rw_gn_b, rw_w_o, sw_w_qkv, sw_b_qkv, sw_sinks, sw_w_o, sw_b_o, sg_w_in, sg_b_in, sg_ln_g, sg_ln_b, sg_w_s, sg_b_s, sg_w_o, sg_b_o, gla_w_in, gla_w_a2, gla_b_a, gla_gn_g, gla_w_o):
    bsz, seq, d = x.shape
    depth = norm_mix.shape[0]
    h = x.reshape(bsz * seq, d)
    g_final = norm_final[None]
    for i in range(depth):
        t, j = i % 4, i // 4
        g_mix = norm_mix[i][None]
        pre = None
        if t == 0:
            pre = _rwkv_layer(h, g_mix, rw_mu[j], rw_w_rkv[j], rw_w0[j], rw_w1[j], rw_w2[j], rw_a0[j], rw_a1[j],
                            rw_a2[j], rw_g1[j], rw_g2[j], rw_k_k[j], rw_k_a[j], rw_r_k[j], rw_gn_g[j], rw_gn_b[j],
                            rw_w_o[j], bsz, seq)
        elif t == 1:
            pre = _swa_layer(h, positions, g_mix, sw_w_qkv[j], sw_b_qkv[j], sw_sinks[j], sw_w_o[j], sw_b_o[j], bsz, seq)
        elif t == 2:
            h = _sgu_layer(h, g_mix, sg_w_in[j], sg_b_in[j], sg_ln_g[j], sg_ln_b[j], sg_w_s[j], sg_b_s[j],
                           sg_w_o[j], sg_b_o[j])
        else:
            h = _gla_layer(h, g_mix, gla_w_in[j], gla_w_a2[j], gla_b_a[j], gla_gn_g[j], gla_w_o[j], bsz, seq)
        h = _ffn(h, norm_ffn[i][None], ffn_w_in, ffn_w_out, g_final, i, pre=pre, final=i == depth - 1)
    return h.reshape(bsz, seq, d)
```

```python
import functools

import jax
import jax.numpy as jnp
from jax import lax
from jax.experimental import pallas as pl
from jax.experimental.pallas import tpu as pltpu

F32 = jnp.float32
BF16 = jnp.bfloat16

HEAD_DIM = 64
NORM_EPS = 1e-5
RW_GN_EPS = 64e-5
SW_Q_HEADS = 16
SW_KV_HEADS = 2
SW_BLOCK = 128
SW_STEP_BLOCKS = 8
ROPE_THETA = 10000.0
SG_CHUNK = 128
SG_GROUPS = 16
GLA_HEADS = 4
GLA_TAU = 16.0
GLA_CHUNK = 64
RW_CHUNK = 64
RW_STEP_CHUNKS = 4

MXU_COLS = 256
LANES = 128
SUBLANES = 8
TOKEN_TILE = 512
SGU_TILE = 1024
VMEM_LIMIT = 56 * 1024 * 1024


def _mm(a, b):
    return jnp.dot(a, b, preferred_element_type=F32)


def _mm_nt(a, b):
    return lax.dot_general(a, b, (((1,), (1,)), ((), ())), preferred_element_type=F32)


def _mm_tn(a, b):
    return lax.dot_general(a, b, (((0,), (0,)), ((), ())), preferred_element_type=F32)


def _split(x):
    hi = x.astype(BF16)
    lo = (x - hi.astype(F32)).astype(BF16)
    return hi, lo


def _mm_rhs2(a, b):
    hi, lo = _split(b)
    return _mm(a, hi) + _mm(a, lo)


def _rms(x, g):
    return x * lax.rsqrt(jnp.mean(x * x, -1, keepdims=True) + NORM_EPS) * g


def _cparams(*sem):
    return pltpu.CompilerParams(dimension_semantics=sem, vmem_limit_bytes=VMEM_LIMIT)


def _full(shape):
    nd = len(shape)
    return pl.BlockSpec(shape, lambda *_: (0,) * nd)


def _rows(tm, n):
    return pl.BlockSpec((tm, n), lambda i: (i, 0))


def _ffn_body(*refs, hidden, fc, pre, final):
    if pre:
        h_ref, a_ref, wo_ref, bo_ref, g_ref, win_ref, wout_ref, gf_ref, o_ref = refs
        x = h_ref[...] + _mm(a_ref[...], wo_ref[...]) + bo_ref[...]
    else:
        h_ref, g_ref, win_ref, wout_ref, gf_ref, o_ref = refs
        x = h_ref[...]
    xn = _rms(x, g_ref[...]).astype(BF16)
    acc = jnp.zeros_like(x)
    for c in range(hidden // fc):
        gate = _mm(xn, win_ref[:, c * fc:(c + 1) * fc].astype(BF16))
        up = _mm(xn, win_ref[:, hidden + c * fc:hidden + (c + 1) * fc].astype(BF16))
        act = (gate * jax.nn.sigmoid(gate) * up).astype(BF16)
        acc = acc + _mm(act, wout_ref[c * fc:(c + 1) * fc, :].astype(BF16))
    y = x + acc
    if final:
        y = _rms(y, gf_ref[...])
    o_ref[...] = y


def _ffn(h, g, w_in, w_out, g_final, layer, *, pre=None, final=False):
    m, d = h.shape
    hidden = w_out.shape[1]
    assert hidden % MXU_COLS == 0 and w_in.shape[2] == 2 * hidden
    tm = TOKEN_TILE
    args, specs = [h], [_rows(tm, d)]
    if pre is not None:
        a, w_o, b_o = pre
        args += [a, w_o, b_o]
        specs += [_rows(tm, a.shape[1]), _full(w_o.shape), _full(b_o.shape)]
    args += [g, w_in, w_out, g_final]
    of_layer = lambda w: pl.BlockSpec((None,) + w.shape[1:], lambda i: (layer, 0, 0), pipeline_mode=pl.Buffered(1))
    specs += [_full(g.shape), of_layer(w_in), of_layer(w_out), _full(g_final.shape)]
    return pl.pallas_call(
        functools.partial(_ffn_body, hidden=hidden, fc=MXU_COLS, pre=pre is not None, final=final),
        out_shape=jax.ShapeDtypeStruct((m, d), F32),
        grid=(m // tm,),
        in_specs=specs,
        out_specs=_rows(tm, d),
        compiler_params=_cparams("parallel"),
        name="ffn",
    )(*args)


def _swa_qkv_body(h_ref, g_ref, w_ref, b_ref, cos_ref, sin_ref, q_ref, k_ref, v_ref, *, qd):
    xn = _rms(h_ref[...], g_ref[...]).astype(BF16)
    qkv = _mm(xn, w_ref[...]) + b_ref[...]
    cos = cos_ref[...]
    sin = sin_ref[...]
    lane = lax.broadcasted_iota(jnp.int32, cos.shape, 1)
    first_half = (lane & (HEAD_DIM - 1)) < HEAD_DIM // 2

    def rope(t):
        rot = jnp.where(first_half, pltpu.roll(t, LANES - HEAD_DIM // 2, 1), pltpu.roll(t, HEAD_DIM // 2, 1))
        return t * cos + rot * sin

    scale = HEAD_DIM ** -0.5
    for j in range(qd // LANES):
        q_ref[:, j * LANES:(j + 1) * LANES] = (rope(qkv[:, j * LANES:(j + 1) * LANES]) * scale).astype(BF16)
    kd = k_ref.shape[1]
    for j in range(kd // LANES):
        k_ref[:, j * LANES:(j + 1) * LANES] = rope(qkv[:, qd + j * LANES:qd + (j + 1) * LANES]).astype(BF16)
    v_ref[...] = qkv[:, qd + kd:].astype(BF16)


def _swa_attn_body(sink_ref, q_ref, kc_ref, kp_ref, vc_ref, vp_ref, o_ref, s_ref, p_ref, rd_ref):
    n = pl.program_id(1)
    blk = SW_BLOCK
    r = lax.broadcasted_iota(jnp.int32, (blk, blk), 0)
    c = lax.broadcasted_iota(jnp.int32, (blk, blk), 1)
    from_prev = c > r
    no_prev = jnp.where(n > 0, 0.0, -jnp.inf)
    lo = lax.broadcasted_iota(jnp.int32, (blk, LANES), 1) < HEAD_DIM
    group = SW_Q_HEADS // SW_KV_HEADS
    zero = jnp.zeros((blk, LANES), BF16)
    halves = lambda t: (jnp.where(lo, t, zero), jnp.where(lo, zero, t))
    kv_cols = [slice(j * LANES, (j + 1) * LANES) for j in range(SW_KV_HEADS)]
    heads = [(j, p, e) for j in range(SW_KV_HEADS) for p in range(group // 2) for e in range(2)]
    q_cols = lambda j, p: slice((j * (group // 2) + p) * LANES, (j * (group // 2) + p + 1) * LANES)
    nblk = q_ref.shape[0] // blk
    rows = [slice(i * blk, (i + 1) * blk) for i in range(nblk)]
    keys = [[halves(kp_ref[:, cj]) for cj in kv_cols]] + [[halves(kc_ref[rs, cj]) for cj in kv_cols] for rs in rows]
    for i, rs in enumerate(rows):
        for h_, (j, p, e) in enumerate(heads):
            qp = q_ref[rs, q_cols(j, p)]
            s_prev = _mm_nt(qp, keys[i][j][e])
            if i == 0:
                s_prev = s_prev + no_prev
            s_ref[i * len(heads) + h_] = jnp.where(from_prev, s_prev, _mm_nt(qp, keys[i + 1][j][e]))
    for i in range(nblk):
        for h_, (j, p, e) in enumerate(heads):
            it = i * len(heads) + h_
            sink = sink_ref[j * group + 2 * p + e]
            s = s_ref[it]
            mx = jnp.maximum(jnp.max(s, -1, keepdims=True), sink)
            pe = jnp.exp(s - mx)
            rd_ref[it] = jnp.broadcast_to(1.0 / (jnp.sum(pe, -1, keepdims=True) + jnp.exp(sink - mx)), (blk, LANES))
            pb = pe.astype(BF16)
            p_ref[it, :, :blk] = jnp.where(from_prev, pb, zero)
            p_ref[it, :, blk:] = jnp.where(from_prev, zero, pb)
    vals = [[halves(vp_ref[:, cj]) for cj in kv_cols]] + [[halves(vc_ref[rs, cj]) for cj in kv_cols] for rs in rows]
    for i, rs in enumerate(rows):
        for j in range(SW_KV_HEADS):
            vcat = [jnp.concatenate([vals[i][j][e], vals[i + 1][j][e]], 0) for e in range(2)]
            for p in range(group // 2):
                it = i * len(heads) + heads.index((j, p, 0))
                o_ref[rs, q_cols(j, p)] = (_mm(p_ref[it], vcat[0]) * rd_ref[it]
                                           + _mm(p_ref[it + 1], vcat[1]) * rd_ref[it + 1]).astype(BF16)


def _swa_layer(h, positions, g_mix, w_qkv, b_qkv, sinks, w_o, b_o, bsz, seq):
    m, d = h.shape
    qd = SW_Q_HEADS * HEAD_DIM
    kd = SW_KV_HEADS * HEAD_DIM
    heads = [slice(base + j * HEAD_DIM, base + (j + 1) * HEAD_DIM)
             for base in (qd, qd + kd) for j in range(SW_KV_HEADS) for _ in (0, 1)]
    w = jnp.concatenate([w_qkv[:, :qd]] + [w_qkv[:, s] for s in heads], 1).astype(BF16)
    b = jnp.concatenate([b_qkv[:qd]] + [b_qkv[s] for s in heads])[None]
    inv_freq = ROPE_THETA ** (-jnp.arange(0, HEAD_DIM, 2, dtype=F32) / HEAD_DIM)
    ang = positions.astype(F32).reshape(m, 1) * inv_freq
    cos = jnp.tile(jnp.cos(ang), (1, 4))
    sin = jnp.tile(jnp.concatenate([-jnp.sin(ang), jnp.sin(ang)], -1), (1, 2))
    tm = TOKEN_TILE
    kdd = 2 * kd
    q, k, v = pl.pallas_call(
        functools.partial(_swa_qkv_body, qd=qd),
        out_shape=(jax.ShapeDtypeStruct((m, qd), BF16), jax.ShapeDtypeStruct((m, kdd), BF16),
                   jax.ShapeDtypeStruct((m, kdd), BF16)),
        grid=(m // tm,),
        in_specs=[_rows(tm, d), _full(g_mix.shape), _full(w.shape), _full(b.shape), _rows(tm, LANES), _rows(tm, LANES)],
        out_specs=(_rows(tm, qd), _rows(tm, kdd), _rows(tm, kdd)),
        compiler_params=_cparams("parallel"),
        name="swa_qkv",
    )(h, g_mix, w, b, cos, sin)
    sb = SW_STEP_BLOCKS
    cur = lambda n_: pl.BlockSpec((None, sb * SW_BLOCK, n_), lambda b_, i: (b_, i, 0))
    prev = lambda n_: pl.BlockSpec((None, SW_BLOCK, n_), lambda b_, i: (b_, jnp.maximum(i * sb - 1, 0), 0))
    k3 = k.reshape(bsz, seq, kdd)
    v3 = v.reshape(bsz, seq, kdd)
    items = sb * SW_Q_HEADS
    o = pl.pallas_call(
        _swa_attn_body,
        out_shape=jax.ShapeDtypeStruct((bsz, seq, qd), BF16),
        grid=(bsz, seq // (sb * SW_BLOCK)),
        in_specs=[pl.BlockSpec(memory_space=pltpu.SMEM), cur(qd), cur(kdd), prev(kdd), cur(kdd), prev(kdd)],
        out_specs=cur(qd),
        scratch_shapes=[
            pltpu.VMEM((items, SW_BLOCK, SW_BLOCK), F32),
            pltpu.VMEM((items, SW_BLOCK, 2 * SW_BLOCK), BF16),
            pltpu.VMEM((items, SW_BLOCK, LANES), F32),
        ],
        compiler_params=_cparams("parallel", "parallel"),
        name="swa_attn",
    )(sinks, q.reshape(bsz, seq, qd), k3, k3, v3, v3)
    return o.reshape(m, qd), w_o.astype(BF16), b_o[None]


def _sgu_body(h_ref, g_ref, win_ref, bin_ref, lng_ref, lnb_ref, ws_ref, bs_ref, wo_ref, bo_ref, o_ref, z_ref, vn_ref, *, width):
    x = h_ref[...]
    tm = x.shape[0]
    xn = _rms(x, g_ref[...]).astype(BF16)
    gelu = lambda t: 0.5 * t * (1.0 + lax.erf(t * (0.5 ** 0.5)))
    v = gelu(_mm(xn, win_ref[:, width:]) + bin_ref[:, width:])
    mu = jnp.mean(v, -1, keepdims=True)
    vc = v - mu
    var = jnp.mean(vc * vc, -1, keepdims=True)
    vn_ref[...] = (vc * lax.rsqrt(var + NORM_EPS) * lng_ref[...] + lnb_ref[...]).astype(BF16)
    gdim = width // SG_GROUPS
    for gp in range(width // MXU_COLS):
        cp = slice(gp * MXU_COLS, (gp + 1) * MXU_COLS)
        u = gelu(_mm(xn, win_ref[:, cp]) + bin_ref[:, cp])
        for half in range(MXU_COLS // gdim):
            gi = gp * (MXU_COLS // gdim) + half
            cs = slice(gi * gdim, (gi + 1) * gdim)
            us = slice(half * gdim, (half + 1) * gdim)
            for q in range(0, tm // SG_CHUNK, 2):
                r0 = slice(q * SG_CHUNK, (q + 1) * SG_CHUNK)
                r1 = slice((q + 1) * SG_CHUNK, (q + 2) * SG_CHUNK)
                sv = _mm(ws_ref[gi], jnp.concatenate([vn_ref[r0, cs], vn_ref[r1, cs]], 1))
                z_ref[r0, cs] = (u[r0, us] * (sv[:, :gdim] + bs_ref[gi])).astype(BF16)
                z_ref[r1, cs] = (u[r1, us] * (sv[:, gdim:] + bs_ref[gi])).astype(BF16)
    o_ref[...] = x + _mm(z_ref[...], wo_ref[...]) + bo_ref[...]


def _sgu_layer(h, g_mix, w_in, b_in, ln_g, ln_b, w_s, b_s, w_o, b_o):
    m, d = h.shape
    width = w_o.shape[0]
    causal = jnp.tril(jnp.ones((SG_CHUNK, SG_CHUNK), dtype=bool))
    ws = jnp.where(causal[None], w_s, 0.0).astype(BF16)
    bs = jnp.broadcast_to(b_s[:, :, None], (SG_GROUPS, SG_CHUNK, width // SG_GROUPS))
    args = (h, g_mix, w_in.astype(BF16), b_in[None], ln_g[None], ln_b[None], ws, bs, w_o.astype(BF16), b_o[None])
    tm = SGU_TILE
    return pl.pallas_call(
        functools.partial(_sgu_body, width=width),
        out_shape=jax.ShapeDtypeStruct((m, d), F32),
        grid=(m // tm,),
        in_specs=[_rows(tm, d)] + [_full(a.shape) for a in args[1:]],
        out_specs=_rows(tm, d),
        scratch_shapes=[pltpu.VMEM((tm, width), BF16), pltpu.VMEM((tm, width), BF16)],
        compiler_params=_cparams("parallel"),
        name="sgu",
    )(*args)


def _gla_body(h_ref, g_ref, win_ref, wa1_ref, wa2_ref, ba_ref, gng_ref, wo_ref, o_ref,
              st_ref, oc_ref, qg_ref, kg_ref, ks_ref, v_ref, dl_ref, eq_ref, ek_ref, es_ref, gt_ref, *, dk, dv):
    @pl.when(pl.program_id(1) == 0)
    def _():
        st_ref[...] = jnp.zeros_like(st_ref)

    tm = h_ref.shape[0]
    hk, hv = dk // GLA_HEADS, dv // GLA_HEADS
    cl = GLA_CHUNK
    heads = [(slice(hh * hk, (hh + 1) * hk), slice(hh * hv, (hh + 1) * hv)) for hh in range(GLA_HEADS)]
    ri = lax.broadcasted_iota(jnp.int32, (cl, cl), 0)
    ci = lax.broadcasted_iota(jnp.int32, (cl, cl), 1)
    causal = ci <= ri
    tri = causal.astype(BF16)
    x = h_ref[...]
    xn = _rms(x, g_ref[...]).astype(BF16)
    chunks = [slice(c * cl, (c + 1) * cl) for c in range(tm // cl)]
    a_low = _mm(xn, wa1_ref[...])
    a_pre = _mm(a_low.astype(BF16), wa2_ref[...]) + ba_ref[...]
    log_a = (jnp.minimum(a_pre, 0.0) - jnp.log1p(jnp.exp(-jnp.abs(a_pre)))) / GLA_TAU
    for c, rs in enumerate(chunks):
        bcum = _mm_rhs2(tri, log_a[rs])
        b_last = bcum[cl - 1:cl]
        eq_ref[rs] = jnp.exp(bcum)
        ek_ref[rs] = jnp.exp(-bcum)
        es_ref[rs] = jnp.exp(b_last - bcum)
        dl_ref[c:c + 1] = jnp.exp(b_last)
    qg_ref[...] = (_mm(xn, win_ref[:, :dk]) * (hk ** -0.5) * eq_ref[...]).astype(BF16)
    k_all = _mm(xn, win_ref[:, dk:2 * dk])
    kg_ref[...] = (k_all * ek_ref[...]).astype(BF16)
    ks_ref[...] = (k_all * es_ref[...]).astype(BF16)
    v_ref[...] = _mm(xn, win_ref[:, 2 * dk:2 * dk + dv]).astype(BF16)
    gate = _mm(xn, win_ref[:, 2 * dk + dv:])
    gt_ref[...] = gate * jax.nn.sigmoid(gate)
    for c, rs in enumerate(chunks):
        for hh, (ks_, vs_) in enumerate(heads):
            vh = v_ref[rs, vs_]
            att = jnp.where(causal, _mm_nt(qg_ref[rs, ks_], kg_ref[rs, ks_]), 0.0)
            st = st_ref[hh]
            oc_ref[rs, vs_] = _mm(att.astype(BF16), vh) + _mm_nt(qg_ref[rs, ks_], st.astype(BF16))
            st_ref[hh] = st * dl_ref[c:c + 1, ks_] + _mm_tn(vh, ks_ref[rs, ks_])
    for hh, (ks_, vs_) in enumerate(heads):
        o = oc_ref[:, vs_]
        o = o * lax.rsqrt(jnp.mean(o * o, -1, keepdims=True) + NORM_EPS) * gng_ref[:, vs_]
        oc_ref[:, vs_] = o * gt_ref[:, vs_]
    o_ref[...] = x + _mm(oc_ref[...].astype(BF16), wo_ref[...])


def _gla_layer(h, g_mix, w_in, w_a2, b_a, gn_g, w_o, bsz, seq):
    m, d = h.shape
    dk = w_a2.shape[1]
    dv = w_o.shape[0]
    lora = w_a2.shape[0]
    main = 2 * dk + 2 * dv
    wa1 = jnp.pad(w_in[:, main:], ((0, 0), (0, LANES - lora))).astype(BF16)
    wa2 = jnp.pad(w_a2, ((0, LANES - lora), (0, 0))).astype(BF16)
    args = (h.reshape(bsz, seq, d), g_mix, w_in[:, :main].astype(BF16), wa1, wa2, b_a[None], gn_g[None], w_o.astype(BF16))
    tm = TOKEN_TILE
    hk, hv = dk // GLA_HEADS, dv // GLA_HEADS
    tile = pl.BlockSpec((None, tm, d), lambda b_, i: (b_, i, 0))
    out = pl.pallas_call(
        functools.partial(_gla_body, dk=dk, dv=dv),
        out_shape=jax.ShapeDtypeStruct((bsz, seq, d), F32),
        grid=(bsz, seq // tm),
        in_specs=[tile] + [_full(a.shape) for a in args[1:]],
        out_specs=tile,
        scratch_shapes=[
            pltpu.VMEM((GLA_HEADS, hv, hk), F32),
            pltpu.VMEM((tm, dv), F32),
            pltpu.VMEM((tm, dk), BF16),
            pltpu.VMEM((tm, dk), BF16),
            pltpu.VMEM((tm, dk), BF16),
            pltpu.VMEM((tm, dv), BF16),
            pltpu.VMEM((tm // GLA_CHUNK, dk), F32),
            pltpu.VMEM((tm, dk), F32),
            pltpu.VMEM((tm, dk), F32),
            pltpu.VMEM((tm, dk), F32),
            pltpu.VMEM((tm, dv), F32),
        ],
        compiler_params=_cparams("parallel", "arbitrary"),
        name="gla",
    )(*args)
    return out.reshape(m, d)


def _rw_proj_body(h_ref, hp_ref, g_ref, mu_ref, wrkv_ref, w0_ref, w1_ref, w2_ref, a0_ref, a1_ref, a2_ref,
                  g1_ref, g2_ref, kk_ref, ka_ref,
                  r_out, ld_out, k_out, v_out, kk_out, a_out, g_out, *, seq):
    g = g_ref[...]
    xn = _rms(h_ref[...], g)
    tm = xn.shape[0]
    starts_seq = lax.rem(pl.program_id(0) * tm, seq) == 0
    prev_row = _rms(hp_ref[...], g)[SUBLANES - 1:SUBLANES] * jnp.where(starts_seq, 0.0, 1.0)
    row = lax.broadcasted_iota(jnp.int32, xn.shape, 0)
    xx = jnp.where(row == 0, prev_row, pltpu.roll(xn, 1, 0)) - xn
    mix = lambda c: (xn + xx * mu_ref[c:c + 1]).astype(BF16)
    w_pre = w0_ref[...] + _mm(jnp.tanh(_mm(mix(1), w1_ref[...])).astype(BF16), w2_ref[...])
    ld_out[...] = -(2.718281828459045 ** -0.5) * jax.nn.sigmoid(w_pre)
    a = jax.nn.sigmoid(a0_ref[...] + _mm(_mm(mix(4), a1_ref[...]).astype(BF16), a2_ref[...]))
    a_out[...] = a.astype(BF16)
    g_out[...] = _mm(jax.nn.sigmoid(_mm(mix(5), g1_ref[...])).astype(BF16), g2_ref[...]).astype(BF16)
    r_out[...] = _mm(mix(0), wrkv_ref[0]).astype(BF16)
    v_out[...] = _mm(mix(3), wrkv_ref[2]).astype(BF16)
    k = _mm(mix(2), wrkv_ref[1])
    k_out[...] = (k * (1.0 + (a - 1.0) * ka_ref[...])).astype(BF16)
    kk_out[...] = (k * kk_ref[...]).astype(BF16)


def _stack_heads(z):
    lo = lax.broadcasted_iota(jnp.int32, z.shape, 1) < HEAD_DIM
    zero = jnp.zeros_like(z)
    return jnp.concatenate([jnp.where(lo, z, zero), jnp.where(lo, zero, z)], 0)


def _rw_scan_body(r_ref, ld_ref, k_ref, v_ref, kk_ref, a_ref, gate_ref, rk_ref, gng_ref, gnb_ref, o_ref,
                  st_ref, cum_ref, lhs_ref, rhs_ref, end_ref, vs_ref, bon_ref, pw_ref, t_ref, ak_ref, rbk_ref,
                  av_ref, g_ref, u_ref, wv_ref, yr_ref):
    @pl.when(pl.program_id(1) == 0)
    def _():
        st_ref[...] = jnp.zeros_like(st_ref)

    cl = RW_CHUNK
    nck = r_ref.shape[0] // cl
    d = r_ref.shape[1]
    npair = d // LANES
    pairs = range(npair)
    items = [(ck, p) for ck in range(nck) for p in pairs]
    c2 = 2 * cl
    bf = lambda t: t.astype(BF16)
    ri = lax.broadcasted_iota(jnp.int32, (cl, cl), 0)
    ci = lax.broadcasted_iota(jnp.int32, (cl, cl), 1)
    tri = (ci <= ri).astype(BF16)
    for ck in range(nck):
        rows = slice(ck * cl, (ck + 1) * cl)
        cum_ref[rows] = _mm_rhs2(tri, ld_ref[rows])

    for ck, p in items:
        q = ck * npair + p
        rows = slice(ck * cl, (ck + 1) * cl)
        cs = slice(p * LANES, (p + 1) * LANES)
        cum = cum_ref[rows, cs]
        cum_end = cum_ref[(ck + 1) * cl - 1:(ck + 1) * cl, cs]
        p_inv = jnp.exp(-cum)
        to_end = jnp.exp(cum_end) * p_inv
        r = r_ref[rows, cs].astype(F32)
        k = k_ref[rows, cs].astype(F32)
        kk = kk_ref[rows, cs].astype(F32)
        kk_st = _stack_heads(kk)
        inv = lax.rsqrt(jnp.maximum(jnp.sum(kk_st * kk_st, -1, keepdims=True), 1e-24))
        kk = kk * jnp.where(lax.broadcasted_iota(jnp.int32, kk.shape, 1) < HEAD_DIM, inv[:cl], inv[cl:])
        b = kk * a_ref[rows, cs].astype(F32)
        vs = _stack_heads(v_ref[rows, cs])
        lhs_ref[q] = jnp.concatenate([_stack_heads(bf(kk * jnp.exp(cum - ld_ref[rows, cs]))),
                                      _stack_heads(bf(r * jnp.exp(cum)))], 0)
        rhs_ref[q] = jnp.concatenate([_stack_heads(bf(b * p_inv)), _stack_heads(bf(k * p_inv))], 0)
        end_ref[q] = jnp.concatenate([_stack_heads(bf(b * to_end)), _stack_heads(bf(k * to_end))], 0)
        vs_ref[q] = vs
        bonus = jnp.sum(_stack_heads(r * k * rk_ref[:, cs]), -1, keepdims=True) * vs.astype(F32)
        bon_ref[q] = bonus[:cl] + bonus[cl:]

    r2 = lax.broadcasted_iota(jnp.int32, (c2, c2), 0)
    q2 = lax.broadcasted_iota(jnp.int32, (c2, c2), 1)
    same = (r2 < cl) == (q2 < cl)
    strict = same & (q2 < r2)
    incl = same & (q2 <= r2)
    eye = (r2 == q2).astype(F32)
    slots = range(nck * npair)
    for q in slots:
        a_all = _mm_nt(lhs_ref[q], rhs_ref[q])
        a_ab = jnp.where(strict, a_all[:c2, :c2], 0.0)
        pw_ref[q] = bf(-a_ab)
        t_ref[q] = eye - a_ab
        ak_ref[q] = bf(jnp.where(strict, a_all[:c2, c2:], 0.0))
        rbk_ref[q, :, :c2] = bf(jnp.where(incl, a_all[c2:, :c2], 0.0))
        rbk_ref[q, :, c2:] = bf(jnp.where(incl, a_all[c2:, c2:], 0.0))

    for q in slots:
        pw = pw_ref[q]
        pw_ref[q] = bf(_mm(pw, pw))
    for _ in range(cl.bit_length() - 3):
        for q in slots:
            pw = pw_ref[q]
            t_inv = t_ref[q]
            both = _mm(pw, jnp.concatenate([pw, bf(t_inv)], 1))
            pw_ref[q] = bf(both[:, :c2])
            t_ref[q] = t_inv + both[:, c2:]
    for q in slots:
        t_inv = t_ref[q]
        t_ref[q] = t_inv + _mm(pw_ref[q], bf(t_inv))
    for q in slots:
        av_ref[q] = bf(_mm(ak_ref[q], vs_ref[q]))
    for q in slots:
        gu = _mm(bf(t_ref[q]), jnp.concatenate([lhs_ref[q, :c2], av_ref[q]], 1))
        g_ref[q] = bf(gu[:, :LANES])
        u_ref[q] = gu[:, LANES:]

    inv_n = 1.0 / HEAD_DIM
    own = lax.broadcasted_iota(jnp.int32, (c2, LANES), 1) // HEAD_DIM == lax.broadcasted_iota(jnp.int32, (c2, LANES), 0) // cl
    for ck in range(nck):
        rows = slice(ck * cl, (ck + 1) * cl)
        for p in pairs:
            q = ck * npair + p
            gr = _mm_nt(jnp.concatenate([g_ref[q], lhs_ref[q, c2:]], 0), bf(st_ref[p]))
            wv_ref[q, :c2] = bf(-(gr[:c2] + u_ref[q]))
            wv_ref[q, c2:] = vs_ref[q]
            yr_ref[q] = gr[c2:]
        for p in pairs:
            q = ck * npair + p
            cs = slice(p * LANES, (p + 1) * LANES)
            wv = wv_ref[q]
            ys = yr_ref[q] + _mm(rbk_ref[q], wv)
            st_ref[p] = st_ref[p] * jnp.exp(cum_ref[(ck + 1) * cl - 1:(ck + 1) * cl, cs]) + _mm_tn(wv, end_ref[q])
            yc = jnp.where(own, ys - jnp.sum(ys, -1, keepdims=True) * inv_n, 0.0)
            yn = yc * lax.rsqrt(jnp.sum(yc * yc, -1, keepdims=True) * inv_n + RW_GN_EPS)
            out = (yn[:cl] + yn[cl:]) * gng_ref[:, cs] + gnb_ref[:, cs] + bon_ref[q]
            o_ref[rows, cs] = (out * gate_ref[rows, cs].astype(F32)).astype(BF16)


def _rwkv_layer(h, g_mix, mu, w_rkv, w0, w1, w2, a0, a1, a2, g1, g2, k_k, k_a, r_k, gn_g, gn_b, w_o, bsz, seq):
    m, d = h.shape
    row = lambda t: t.reshape(1, -1)
    tm = TOKEN_TILE
    proj_args = (h, h, g_mix, mu, w_rkv.astype(BF16), row(w0), w1.astype(BF16), w2.astype(BF16), row(a0),
                 a1.astype(BF16), a2.astype(BF16), g1.astype(BF16), g2.astype(BF16), row(k_k), row(k_a))
    before = pl.BlockSpec((SUBLANES, d), lambda i: (jnp.maximum(i * (tm // SUBLANES) - 1, 0), 0))
    dts = (BF16, F32, BF16, BF16, BF16, BF16, BF16)
    outs = pl.pallas_call(
        functools.partial(_rw_proj_body, seq=seq),
        out_shape=tuple(jax.ShapeDtypeStruct((m, d), dt) for dt in dts),
        grid=(m // tm,),
        in_specs=[_rows(tm, d), before] + [_full(a.shape) for a in proj_args[2:]],
        out_specs=tuple(_rows(tm, d) for _ in dts),
        compiler_params=_cparams("parallel"),
        name="rwkv_proj",
    )(*proj_args)
    cl = RW_CHUNK
    c2 = 2 * cl
    npair = d // LANES
    rows = RW_STEP_CHUNKS * cl
    ns = RW_STEP_CHUNKS * npair
    blk = pl.BlockSpec((None, rows, d), lambda b_, i: (b_, i, 0))
    vec = (row(r_k), row(gn_g), row(gn_b))
    o = pl.pallas_call(
        _rw_scan_body,
        out_shape=jax.ShapeDtypeStruct((bsz, seq, d), BF16),
        grid=(bsz, seq // rows),
        in_specs=[blk] * 7 + [pl.BlockSpec(a.shape, lambda b_, i: (0, 0)) for a in vec],
        out_specs=blk,
        scratch_shapes=[
            pltpu.VMEM((npair, c2, LANES), F32),
            pltpu.VMEM((rows, d), F32),
            pltpu.VMEM((ns, 2 * c2, LANES), BF16),
            pltpu.VMEM((ns, 2 * c2, LANES), BF16),
            pltpu.VMEM((ns, 2 * c2, LANES), BF16),
            pltpu.VMEM((ns, c2, LANES), BF16),
            pltpu.VMEM((ns, cl, LANES), F32),
            pltpu.VMEM((ns, c2, c2), BF16),
            pltpu.VMEM((ns, c2, c2), F32),
            pltpu.VMEM((ns, c2, c2), BF16),
            pltpu.VMEM((ns, c2, 2 * c2), BF16),
            pltpu.VMEM((ns, c2, LANES), BF16),
            pltpu.VMEM((ns, c2, LANES), BF16),
            pltpu.VMEM((ns, c2, LANES), F32),
            pltpu.VMEM((ns, 2 * c2, LANES), BF16),
            pltpu.VMEM((ns, c2, LANES), F32),
        ],
        compiler_params=_cparams("parallel", "arbitrary"),
        name="rwkv_scan",
    )(*[t.reshape(bsz, seq, d) for t in outs], *vec)
    return o.reshape(m, d), w_o.astype(BF16), jnp.zeros((1, d), F32)


def kernel(x, positions, norm_mix, norm_ffn, ffn_w_in, ffn_w_out, norm_final, rw_mu, rw_w_rkv, rw_w0, rw_w1, rw_w2, rw_a0, rw_a1, rw_a2, rw_g1, rw_g2, rw_k_k, rw_k_a, rw_r_k, rw_gn_g, rw_gn_b, rw_w_o, sw_w_qkv, sw_b_qkv, sw_sinks, sw_w_o, sw_b_o, sg_w_in, sg_b_in, sg_ln_g, sg_ln_b, sg_w_s, sg_b_s, sg_w_o, sg_b_o, gla_w_in, gla_w_a2, gla_b_a, gla_gn_g, gla_w_o):
    bsz, seq, d = x.shape
    depth = norm_mix.shape[0]
    h = x.reshape(bsz * seq, d)
    g_final = norm_final[None]
    for i in range(depth):
        t, j = i % 4, i // 4
        g_mix = norm_mix[i][None]
        pre = None
        if t == 0:
            pre = _rwkv_layer(h, g_mix, rw_mu[j], rw_w_rkv[j], rw_w0[j], rw_w1[j], rw_w2[j], rw_a0[j], rw_a1[j],
                            rw_a2[j], rw_g1[j], rw_g2[j], rw_k_k[j], rw_k_a[j], rw_r_k[j], rw_gn_g[j], rw_gn_b[j],
                            rw_w_o[j], bsz, seq)
        elif t == 1:
            pre = _swa_layer(h, positions, g_mix, sw_w_qkv[j], sw_b_qkv[j], sw_sinks[j], sw_w_o[j], sw_b_o[j], bsz, seq)
        elif t == 2:
            h = _sgu_layer(h, g_mix, sg_w_in[j], sg_b_in[j], sg_ln_g[j], sg_ln_b[j], sg_w_s[j], sg_b_s[j],
                           sg_w_o[j], sg_b_o[j])
        else:
            h = _gla_layer(h, g_mix, gla_w_in[j], gla_w_a2[j], gla_b_a[j], gla_gn_g[j], gla_w_o[j], bsz, seq)
        h = _ffn(h, norm_ffn[i][None], ffn_w_in, ffn_w_out, g_final, i, pre=pre, final=i == depth - 1)
    return h.reshape(bsz, seq, d)
```

```python
import functools

import jax
import jax.numpy as jnp
from jax import lax
from jax.experimental import pallas as pl
from jax.experimental.pallas import tpu as pltpu

F32 = jnp.float32
BF16 = jnp.bfloat16

HEAD_DIM = 64
NORM_EPS = 1e-5
RW_GN_EPS = 64e-5
SW_Q_HEADS = 16
SW_KV_HEADS = 2
SW_BLOCK = 128
SW_STEP_BLOCKS = 8
ROPE_THETA = 10000.0
SG_CHUNK = 128
SG_GROUPS = 16
GLA_HEADS = 4
GLA_TAU = 16.0
GLA_CHUNK = 64
RW_CHUNK = 64
RW_STEP_CHUNKS = 4

MXU_COLS = 256
LANES = 128
SUBLANES = 8
TOKEN_TILE = 512
WIDE_TILE = 1024
VMEM_LIMIT = 56 * 1024 * 1024


def _mm(a, b):
    return jnp.dot(a, b, preferred_element_type=F32)


def _mm_nt(a, b):
    return lax.dot_general(a, b, (((1,), (1,)), ((), ())), preferred_element_type=F32)


def _mm_tn(a, b):
    return lax.dot_general(a, b, (((0,), (0,)), ((), ())), preferred_element_type=F32)


def _split(x):
    hi = x.astype(BF16)
    lo = (x - hi.astype(F32)).astype(BF16)
    return hi, lo


def _mm_rhs2(a, b):
    hi, lo = _split(b)
    return _mm(a, hi) + _mm(a, lo)


def _rms(x, g):
    return x * lax.rsqrt(jnp.mean(x * x, -1, keepdims=True) + NORM_EPS) * g


def _cparams(*sem):
    return pltpu.CompilerParams(dimension_semantics=sem, vmem_limit_bytes=VMEM_LIMIT)


def _full(shape):
    nd = len(shape)
    return pl.BlockSpec(shape, lambda *_: (0,) * nd)


def _rows(tm, n):
    return pl.BlockSpec((tm, n), lambda i: (i, 0))


def _ffn_body(*refs, hidden, fc, pre, final):
    if pre:
        h_ref, a_ref, wo_ref, bo_ref, g_ref, win_ref, wout_ref, gf_ref, o_ref = refs
        x = h_ref[...] + _mm(a_ref[...], wo_ref[...]) + bo_ref[...]
    else:
        h_ref, g_ref, win_ref, wout_ref, gf_ref, o_ref = refs
        x = h_ref[...]
    xn = _rms(x, g_ref[...]).astype(BF16)
    acc = jnp.zeros_like(x)
    for c in range(hidden // fc):
        gate = _mm(xn, win_ref[:, c * fc:(c + 1) * fc].astype(BF16))
        up = _mm(xn, win_ref[:, hidden + c * fc:hidden + (c + 1) * fc].astype(BF16))
        act = (gate * jax.nn.sigmoid(gate) * up).astype(BF16)
        acc = acc + _mm(act, wout_ref[c * fc:(c + 1) * fc, :].astype(BF16))
    y = x + acc
    if final:
        y = _rms(y, gf_ref[...])
    o_ref[...] = y


def _ffn(h, g, w_in, w_out, g_final, layer, *, pre=None, final=False):
    m, d = h.shape
    hidden = w_out.shape[1]
    assert hidden % MXU_COLS == 0 and w_in.shape[2] == 2 * hidden
    tm = TOKEN_TILE
    args, specs = [h], [_rows(tm, d)]
    if pre is not None:
        a, w_o, b_o = pre
        args += [a, w_o, b_o]
        specs += [_rows(tm, a.shape[1]), _full(w_o.shape), _full(b_o.shape)]
    args += [g, w_in, w_out, g_final]
    of_layer = lambda w: pl.BlockSpec((None,) + w.shape[1:], lambda i: (layer, 0, 0), pipeline_mode=pl.Buffered(1))
    specs += [_full(g.shape), of_layer(w_in), of_layer(w_out), _full(g_final.shape)]
    return pl.pallas_call(
        functools.partial(_ffn_body, hidden=hidden, fc=MXU_COLS, pre=pre is not None, final=final),
        out_shape=jax.ShapeDtypeStruct((m, d), F32),
        grid=(m // tm,),
        in_specs=specs,
        out_specs=_rows(tm, d),
        compiler_params=_cparams("parallel"),
        name="ffn",
    )(*args)


def _swa_qkv_body(h_ref, g_ref, w_ref, b_ref, cos_ref, sin_ref, q_ref, k_ref, v_ref, *, qd):
    xn = _rms(h_ref[...], g_ref[...]).astype(BF16)
    qkv = _mm(xn, w_ref[...]) + b_ref[...]
    cos = cos_ref[...]
    sin = sin_ref[...]
    lane = lax.broadcasted_iota(jnp.int32, cos.shape, 1)
    first_half = (lane & (HEAD_DIM - 1)) < HEAD_DIM // 2

    def rope(t):
        rot = jnp.where(first_half, pltpu.roll(t, LANES - HEAD_DIM // 2, 1), pltpu.roll(t, HEAD_DIM // 2, 1))
        return t * cos + rot * sin

    scale = HEAD_DIM ** -0.5
    for j in range(qd // LANES):
        q_ref[:, j * LANES:(j + 1) * LANES] = (rope(qkv[:, j * LANES:(j + 1) * LANES]) * scale).astype(BF16)
    kd = k_ref.shape[1]
    for j in range(kd // LANES):
        k_ref[:, j * LANES:(j + 1) * LANES] = rope(qkv[:, qd + j * LANES:qd + (j + 1) * LANES]).astype(BF16)
    v_ref[...] = qkv[:, qd + kd:].astype(BF16)


def _swa_attn_body(sink_ref, q_ref, kc_ref, kp_ref, vc_ref, vp_ref, o_ref, s_ref, p_ref, rd_ref):
    n = pl.program_id(1)
    blk = SW_BLOCK
    r = lax.broadcasted_iota(jnp.int32, (blk, blk), 0)
    c = lax.broadcasted_iota(jnp.int32, (blk, blk), 1)
    from_prev = c > r
    no_prev = jnp.where(n > 0, 0.0, -jnp.inf)
    lo = lax.broadcasted_iota(jnp.int32, (blk, LANES), 1) < HEAD_DIM
    group = SW_Q_HEADS // SW_KV_HEADS
    zero = jnp.zeros((blk, LANES), BF16)
    halves = lambda t: (jnp.where(lo, t, zero), jnp.where(lo, zero, t))
    kv_cols = [slice(j * LANES, (j + 1) * LANES) for j in range(SW_KV_HEADS)]
    heads = [(j, p, e) for j in range(SW_KV_HEADS) for p in range(group // 2) for e in range(2)]
    q_cols = lambda j, p: slice((j * (group // 2) + p) * LANES, (j * (group // 2) + p + 1) * LANES)
    nblk = q_ref.shape[0] // blk
    rows = [slice(i * blk, (i + 1) * blk) for i in range(nblk)]
    keys = [[halves(kp_ref[:, cj]) for cj in kv_cols]] + [[halves(kc_ref[rs, cj]) for cj in kv_cols] for rs in rows]
    for i, rs in enumerate(rows):
        for h_, (j, p, e) in enumerate(heads):
            qp = q_ref[rs, q_cols(j, p)]
            s_prev = _mm_nt(qp, keys[i][j][e])
            if i == 0:
                s_prev = s_prev + no_prev
            s_ref[i * len(heads) + h_] = jnp.where(from_prev, s_prev, _mm_nt(qp, keys[i + 1][j][e]))
    for i in range(nblk):
        for h_, (j, p, e) in enumerate(heads):
            it = i * len(heads) + h_
            sink = sink_ref[j * group + 2 * p + e]
            s = s_ref[it]
            mx = jnp.maximum(jnp.max(s, -1, keepdims=True), sink)
            pe = jnp.exp(s - mx)
            rd_ref[it] = jnp.broadcast_to(1.0 / (jnp.sum(pe, -1, keepdims=True) + jnp.exp(sink - mx)), (blk, LANES))
            pb = pe.astype(BF16)
            p_ref[it, :, :blk] = jnp.where(from_prev, pb, zero)
            p_ref[it, :, blk:] = jnp.where(from_prev, zero, pb)
    vals = [[halves(vp_ref[:, cj]) for cj in kv_cols]] + [[halves(vc_ref[rs, cj]) for cj in kv_cols] for rs in rows]
    for i, rs in enumerate(rows):
        for j in range(SW_KV_HEADS):
            vcat = [jnp.concatenate([vals[i][j][e], vals[i + 1][j][e]], 0) for e in range(2)]
            for p in range(group // 2):
                it = i * len(heads) + heads.index((j, p, 0))
                o_ref[rs, q_cols(j, p)] = (_mm(p_ref[it], vcat[0]) * rd_ref[it]
                                           + _mm(p_ref[it + 1], vcat[1]) * rd_ref[it + 1]).astype(BF16)


def _swa_layer(h, positions, g_mix, w_qkv, b_qkv, sinks, w_o, b_o, bsz, seq):
    m, d = h.shape
    qd = SW_Q_HEADS * HEAD_DIM
    kd = SW_KV_HEADS * HEAD_DIM
    heads = [slice(base + j * HEAD_DIM, base + (j + 1) * HEAD_DIM)
             for base in (qd, qd + kd) for j in range(SW_KV_HEADS) for _ in (0, 1)]
    w = jnp.concatenate([w_qkv[:, :qd]] + [w_qkv[:, s] for s in heads], 1).astype(BF16)
    b = jnp.concatenate([b_qkv[:qd]] + [b_qkv[s] for s in heads])[None]
    inv_freq = ROPE_THETA ** (-jnp.arange(0, HEAD_DIM, 2, dtype=F32) / HEAD_DIM)
    ang = positions.astype(F32).reshape(m, 1) * inv_freq
    cos = jnp.tile(jnp.cos(ang), (1, 4))
    sin = jnp.tile(jnp.concatenate([-jnp.sin(ang), jnp.sin(ang)], -1), (1, 2))
    tm = WIDE_TILE
    kdd = 2 * kd
    q, k, v = pl.pallas_call(
        functools.partial(_swa_qkv_body, qd=qd),
        out_shape=(jax.ShapeDtypeStruct((m, qd), BF16), jax.ShapeDtypeStruct((m, kdd), BF16),
                   jax.ShapeDtypeStruct((m, kdd), BF16)),
        grid=(m // tm,),
        in_specs=[_rows(tm, d), _full(g_mix.shape), _full(w.shape), _full(b.shape), _rows(tm, LANES), _rows(tm, LANES)],
        out_specs=(_rows(tm, qd), _rows(tm, kdd), _rows(tm, kdd)),
        compiler_params=_cparams("parallel"),
        name="swa_qkv",
    )(h, g_mix, w, b, cos, sin)
    sb = SW_STEP_BLOCKS
    cur = lambda n_: pl.BlockSpec((None, sb * SW_BLOCK, n_), lambda b_, i: (b_, i, 0))
    prev = lambda n_: pl.BlockSpec((None, SW_BLOCK, n_), lambda b_, i: (b_, jnp.maximum(i * sb - 1, 0), 0))
    k3 = k.reshape(bsz, seq, kdd)
    v3 = v.reshape(bsz, seq, kdd)
    items = sb * SW_Q_HEADS
    o = pl.pallas_call(
        _swa_attn_body,
        out_shape=jax.ShapeDtypeStruct((bsz, seq, qd), BF16),
        grid=(bsz, seq // (sb * SW_BLOCK)),
        in_specs=[pl.BlockSpec(memory_space=pltpu.SMEM), cur(qd), cur(kdd), prev(kdd), cur(kdd), prev(kdd)],
        out_specs=cur(qd),
        scratch_shapes=[
            pltpu.VMEM((items, SW_BLOCK, SW_BLOCK), F32),
            pltpu.VMEM((items, SW_BLOCK, 2 * SW_BLOCK), BF16),
            pltpu.VMEM((items, SW_BLOCK, LANES), F32),
        ],
        compiler_params=_cparams("parallel", "parallel"),
        name="swa_attn",
    )(sinks, q.reshape(bsz, seq, qd), k3, k3, v3, v3)
    return o.reshape(m, qd), w_o.astype(BF16), b_o[None]


def _sgu_body(h_ref, g_ref, win_ref, bin_ref, lng_ref, lnb_ref, ws_ref, bs_ref, wo_ref, bo_ref, o_ref, z_ref, vn_ref, *, width):
    x = h_ref[...]
    tm = x.shape[0]
    xn = _rms(x, g_ref[...]).astype(BF16)
    gelu = lambda t: 0.5 * t * (1.0 + lax.erf(t * (0.5 ** 0.5)))
    v = gelu(_mm(xn, win_ref[:, width:]) + bin_ref[:, width:])
    mu = jnp.mean(v, -1, keepdims=True)
    vc = v - mu
    var = jnp.mean(vc * vc, -1, keepdims=True)
    vn_ref[...] = (vc * lax.rsqrt(var + NORM_EPS) * lng_ref[...] + lnb_ref[...]).astype(BF16)
    gdim = width // SG_GROUPS
    for gp in range(width // MXU_COLS):
        cp = slice(gp * MXU_COLS, (gp + 1) * MXU_COLS)
        u = gelu(_mm(xn, win_ref[:, cp]) + bin_ref[:, cp])
        for half in range(MXU_COLS // gdim):
            gi = gp * (MXU_COLS // gdim) + half
            cs = slice(gi * gdim, (gi + 1) * gdim)
            us = slice(half * gdim, (half + 1) * gdim)
            for q in range(0, tm // SG_CHUNK, 2):
                r0 = slice(q * SG_CHUNK, (q + 1) * SG_CHUNK)
                r1 = slice((q + 1) * SG_CHUNK, (q + 2) * SG_CHUNK)
                sv = _mm(ws_ref[gi], jnp.concatenate([vn_ref[r0, cs], vn_ref[r1, cs]], 1))
                z_ref[r0, cs] = (u[r0, us] * (sv[:, :gdim] + bs_ref[gi])).astype(BF16)
                z_ref[r1, cs] = (u[r1, us] * (sv[:, gdim:] + bs_ref[gi])).astype(BF16)
    o_ref[...] = x + _mm(z_ref[...], wo_ref[...]) + bo_ref[...]


def _sgu_layer(h, g_mix, w_in, b_in, ln_g, ln_b, w_s, b_s, w_o, b_o):
    m, d = h.shape
    width = w_o.shape[0]
    causal = jnp.tril(jnp.ones((SG_CHUNK, SG_CHUNK), dtype=bool))
    ws = jnp.where(causal[None], w_s, 0.0).astype(BF16)
    bs = jnp.broadcast_to(b_s[:, :, None], (SG_GROUPS, SG_CHUNK, width // SG_GROUPS))
    args = (h, g_mix, w_in.astype(BF16), b_in[None], ln_g[None], ln_b[None], ws, bs, w_o.astype(BF16), b_o[None])
    tm = WIDE_TILE
    return pl.pallas_call(
        functools.partial(_sgu_body, width=width),
        out_shape=jax.ShapeDtypeStruct((m, d), F32),
        grid=(m // tm,),
        in_specs=[_rows(tm, d)] + [_full(a.shape) for a in args[1:]],
        out_specs=_rows(tm, d),
        scratch_shapes=[pltpu.VMEM((tm, width), BF16), pltpu.VMEM((tm, width), BF16)],
        compiler_params=_cparams("parallel"),
        name="sgu",
    )(*args)


def _gla_body(h_ref, g_ref, win_ref, wa1_ref, wa2_ref, ba_ref, gng_ref, wo_ref, o_ref,
              st_ref, oc_ref, qg_ref, kg_ref, ks_ref, v_ref, dl_ref, eq_ref, ek_ref, es_ref, gt_ref, att_ref, kv_ref, *, dk, dv):
    @pl.when(pl.program_id(1) == 0)
    def _():
        st_ref[...] = jnp.zeros_like(st_ref)

    tm = h_ref.shape[0]
    hk, hv = dk // GLA_HEADS, dv // GLA_HEADS
    cl = GLA_CHUNK
    heads = [(slice(hh * hk, (hh + 1) * hk), slice(hh * hv, (hh + 1) * hv)) for hh in range(GLA_HEADS)]
    ri = lax.broadcasted_iota(jnp.int32, (cl, cl), 0)
    ci = lax.broadcasted_iota(jnp.int32, (cl, cl), 1)
    causal = ci <= ri
    tri = causal.astype(BF16)
    x = h_ref[...]
    xn = _rms(x, g_ref[...]).astype(BF16)
    chunks = [slice(c * cl, (c + 1) * cl) for c in range(tm // cl)]
    a_low = _mm(xn, wa1_ref[...])
    a_pre = _mm(a_low.astype(BF16), wa2_ref[...]) + ba_ref[...]
    log_a = (jnp.minimum(a_pre, 0.0) - jnp.log1p(jnp.exp(-jnp.abs(a_pre)))) / GLA_TAU
    for c, rs in enumerate(chunks):
        bcum = _mm_rhs2(tri, log_a[rs])
        b_last = bcum[cl - 1:cl]
        eq_ref[rs] = jnp.exp(bcum)
        ek_ref[rs] = jnp.exp(-bcum)
        es_ref[rs] = jnp.exp(b_last - bcum)
        dl_ref[c:c + 1] = jnp.exp(b_last)
    qg_ref[...] = (_mm(xn, win_ref[:, :dk]) * (hk ** -0.5) * eq_ref[...]).astype(BF16)
    k_all = _mm(xn, win_ref[:, dk:2 * dk])
    kg_ref[...] = (k_all * ek_ref[...]).astype(BF16)
    ks_ref[...] = (k_all * es_ref[...]).astype(BF16)
    v_ref[...] = _mm(xn, win_ref[:, 2 * dk:2 * dk + dv]).astype(BF16)
    gate = _mm(xn, win_ref[:, 2 * dk + dv:])
    gt_ref[...] = gate * jax.nn.sigmoid(gate)
    for c, rs in enumerate(chunks):
        for hh, (ks_, vs_) in enumerate(heads):
            att_ref[c, hh] = jnp.where(causal, _mm_nt(qg_ref[rs, ks_], kg_ref[rs, ks_]), 0.0).astype(BF16)
    for c, rs in enumerate(chunks):
        for hh, (ks_, vs_) in enumerate(heads):
            oc_ref[rs, vs_] = _mm(att_ref[c, hh], v_ref[rs, vs_])
    for c, rs in enumerate(chunks):
        for hh, (ks_, vs_) in enumerate(heads):
            kv_ref[c, hh] = _mm_tn(v_ref[rs, vs_], ks_ref[rs, ks_])
    for c, rs in enumerate(chunks):
        for hh, (ks_, vs_) in enumerate(heads):
            st = st_ref[hh]
            oc_ref[rs, vs_] += _mm_nt(qg_ref[rs, ks_], st.astype(BF16))
            st_ref[hh] = st * dl_ref[c:c + 1, ks_] + kv_ref[c, hh]
    for hh, (ks_, vs_) in enumerate(heads):
        o = oc_ref[:, vs_]
        o = o * lax.rsqrt(jnp.mean(o * o, -1, keepdims=True) + NORM_EPS) * gng_ref[:, vs_]
        oc_ref[:, vs_] = o * gt_ref[:, vs_]
    o_ref[...] = x + _mm(oc_ref[...].astype(BF16), wo_ref[...])


def _gla_layer(h, g_mix, w_in, w_a2, b_a, gn_g, w_o, bsz, seq):
    m, d = h.shape
    dk = w_a2.shape[1]
    dv = w_o.shape[0]
    lora = w_a2.shape[0]
    main = 2 * dk + 2 * dv
    wa1 = jnp.pad(w_in[:, main:], ((0, 0), (0, LANES - lora))).astype(BF16)
    wa2 = jnp.pad(w_a2, ((0, LANES - lora), (0, 0))).astype(BF16)
    args = (h.reshape(bsz, seq, d), g_mix, w_in[:, :main].astype(BF16), wa1, wa2, b_a[None], gn_g[None], w_o.astype(BF16))
    tm = TOKEN_TILE
    hk, hv = dk // GLA_HEADS, dv // GLA_HEADS
    tile = pl.BlockSpec((None, tm, d), lambda b_, i: (b_, i, 0))
    out = pl.pallas_call(
        functools.partial(_gla_body, dk=dk, dv=dv),
        out_shape=jax.ShapeDtypeStruct((bsz, seq, d), F32),
        grid=(bsz, seq // tm),
        in_specs=[tile] + [_full(a.shape) for a in args[1:]],
        out_specs=tile,
        scratch_shapes=[
            pltpu.VMEM((GLA_HEADS, hv, hk), F32),
            pltpu.VMEM((tm, dv), F32),
            pltpu.VMEM((tm, dk), BF16),
            pltpu.VMEM((tm, dk), BF16),
            pltpu.VMEM((tm, dk), BF16),
            pltpu.VMEM((tm, dv), BF16),
            pltpu.VMEM((tm // GLA_CHUNK, dk), F32),
            pltpu.VMEM((tm, dk), F32),
            pltpu.VMEM((tm, dk), F32),
            pltpu.VMEM((tm, dk), F32),
            pltpu.VMEM((tm, dv), F32),
            pltpu.VMEM((tm // GLA_CHUNK, GLA_HEADS, GLA_CHUNK, GLA_CHUNK), BF16),
            pltpu.VMEM((tm // GLA_CHUNK, GLA_HEADS, hv, hk), F32),
        ],
        compiler_params=_cparams("parallel", "arbitrary"),
        name="gla",
    )(*args)
    return out.reshape(m, d)


def _rw_proj_body(h_ref, hp_ref, g_ref, mu_ref, wrkv_ref, w0_ref, w1_ref, w2_ref, a0_ref, a1_ref, a2_ref,
                  g1_ref, g2_ref, kk_ref, ka_ref,
                  r_out, ld_out, k_out, v_out, kk_out, a_out, g_out, *, seq):
    g = g_ref[...]
    xn = _rms(h_ref[...], g)
    tm = xn.shape[0]
    starts_seq = lax.rem(pl.program_id(0) * tm, seq) == 0
    prev_row = _rms(hp_ref[...], g)[SUBLANES - 1:SUBLANES] * jnp.where(starts_seq, 0.0, 1.0)
    row = lax.broadcasted_iota(jnp.int32, xn.shape, 0)
    xx = jnp.where(row == 0, prev_row, pltpu.roll(xn, 1, 0)) - xn
    mix = lambda c: (xn + xx * mu_ref[c:c + 1]).astype(BF16)
    w_pre = w0_ref[...] + _mm(jnp.tanh(_mm(mix(1), w1_ref[...])).astype(BF16), w2_ref[...])
    ld_out[...] = -(2.718281828459045 ** -0.5) * jax.nn.sigmoid(w_pre)
    a = jax.nn.sigmoid(a0_ref[...] + _mm(_mm(mix(4), a1_ref[...]).astype(BF16), a2_ref[...]))
    a_out[...] = a.astype(BF16)
    g_out[...] = _mm(jax.nn.sigmoid(_mm(mix(5), g1_ref[...])).astype(BF16), g2_ref[...]).astype(BF16)
    r_out[...] = _mm(mix(0), wrkv_ref[0]).astype(BF16)
    v_out[...] = _mm(mix(3), wrkv_ref[2]).astype(BF16)
    k = _mm(mix(2), wrkv_ref[1])
    k_out[...] = (k * (1.0 + (a - 1.0) * ka_ref[...])).astype(BF16)
    kk_out[...] = (k * kk_ref[...]).astype(BF16)


def _stack_heads(z):
    lo = lax.broadcasted_iota(jnp.int32, z.shape, 1) < HEAD_DIM
    zero = jnp.zeros_like(z)
    return jnp.concatenate([jnp.where(lo, z, zero), jnp.where(lo, zero, z)], 0)


def _rw_scan_body(r_ref, ld_ref, k_ref, v_ref, kk_ref, a_ref, gate_ref, rk_ref, gng_ref, gnb_ref, o_ref,
                  st_ref, cum_ref, lhs_ref, rhs_ref, end_ref, vs_ref, bon_ref, pw_ref, t_ref, ak_ref, rbk_ref,
                  av_ref, g_ref, u_ref, wv_ref, yr_ref):
    @pl.when(pl.program_id(1) == 0)
    def _():
        st_ref[...] = jnp.zeros_like(st_ref)

    cl = RW_CHUNK
    nck = r_ref.shape[0] // cl
    d = r_ref.shape[1]
    npair = d // LANES
    pairs = range(npair)
    items = [(ck, p) for ck in range(nck) for p in pairs]
    c2 = 2 * cl
    bf = lambda t: t.astype(BF16)
    ri = lax.broadcasted_iota(jnp.int32, (cl, cl), 0)
    ci = lax.broadcasted_iota(jnp.int32, (cl, cl), 1)
    tri = (ci <= ri).astype(BF16)
    for ck in range(nck):
        rows = slice(ck * cl, (ck + 1) * cl)
        cum_ref[rows] = _mm_rhs2(tri, ld_ref[rows])

    for ck, p in items:
        q = ck * npair + p
        rows = slice(ck * cl, (ck + 1) * cl)
        cs = slice(p * LANES, (p + 1) * LANES)
        cum = cum_ref[rows, cs]
        cum_end = cum_ref[(ck + 1) * cl - 1:(ck + 1) * cl, cs]
        p_inv = jnp.exp(-cum)
        to_end = jnp.exp(cum_end) * p_inv
        r = r_ref[rows, cs].astype(F32)
        k = k_ref[rows, cs].astype(F32)
        kk = kk_ref[rows, cs].astype(F32)
        kk_st = _stack_heads(kk)
        inv = lax.rsqrt(jnp.maximum(jnp.sum(kk_st * kk_st, -1, keepdims=True), 1e-24))
        kk = kk * jnp.where(lax.broadcasted_iota(jnp.int32, kk.shape, 1) < HEAD_DIM, inv[:cl], inv[cl:])
        b = kk * a_ref[rows, cs].astype(F32)
        vs = _stack_heads(v_ref[rows, cs])
        lhs_ref[q] = jnp.concatenate([_stack_heads(bf(kk * jnp.exp(cum - ld_ref[rows, cs]))),
                                      _stack_heads(bf(r * jnp.exp(cum)))], 0)
        rhs_ref[q] = jnp.concatenate([_stack_heads(bf(b * p_inv)), _stack_heads(bf(k * p_inv))], 0)
        end_ref[q] = jnp.concatenate([_stack_heads(bf(b * to_end)), _stack_heads(bf(k * to_end))], 0)
        vs_ref[q] = vs
        bonus = jnp.sum(_stack_heads(r * k * rk_ref[:, cs]), -1, keepdims=True) * vs.astype(F32)
        bon_ref[q] = bonus[:cl] + bonus[cl:]

    r2 = lax.broadcasted_iota(jnp.int32, (c2, c2), 0)
    q2 = lax.broadcasted_iota(jnp.int32, (c2, c2), 1)
    same = (r2 < cl) == (q2 < cl)
    strict = same & (q2 < r2)
    incl = same & (q2 <= r2)
    eye = (r2 == q2).astype(F32)
    slots = range(nck * npair)
    for q in slots:
        a_all = _mm_nt(lhs_ref[q], rhs_ref[q])
        a_ab = jnp.where(strict, a_all[:c2, :c2], 0.0)
        pw_ref[q] = bf(-a_ab)
        t_ref[q] = eye - a_ab
        ak_ref[q] = bf(jnp.where(strict, a_all[:c2, c2:], 0.0))
        rbk_ref[q, :, :c2] = bf(jnp.where(incl, a_all[c2:, :c2], 0.0))
        rbk_ref[q, :, c2:] = bf(jnp.where(incl, a_all[c2:, c2:], 0.0))

    for q in slots:
        pw = pw_ref[q]
        pw_ref[q] = bf(_mm(pw, pw))
    for _ in range(cl.bit_length() - 3):
        for q in slots:
            pw = pw_ref[q]
            t_inv = t_ref[q]
            both = _mm(pw, jnp.concatenate([pw, bf(t_inv)], 1))
            pw_ref[q] = bf(both[:, :c2])
            t_ref[q] = t_inv + both[:, c2:]
    for q in slots:
        t_inv = t_ref[q]
        t_ref[q] = t_inv + _mm(pw_ref[q], bf(t_inv))
    for q in slots:
        av_ref[q] = bf(_mm(ak_ref[q], vs_ref[q]))
    for q in slots:
        gu = _mm(bf(t_ref[q]), jnp.concatenate([lhs_ref[q, :c2], av_ref[q]], 1))
        g_ref[q] = bf(gu[:, :LANES])
        u_ref[q] = gu[:, LANES:]

    inv_n = 1.0 / HEAD_DIM
    own = lax.broadcasted_iota(jnp.int32, (c2, LANES), 1) // HEAD_DIM == lax.broadcasted_iota(jnp.int32, (c2, LANES), 0) // cl
    for ck in range(nck):
        rows = slice(ck * cl, (ck + 1) * cl)
        for p in pairs:
            q = ck * npair + p
            gr = _mm_nt(jnp.concatenate([g_ref[q], lhs_ref[q, c2:]], 0), bf(st_ref[p]))
            wv_ref[q, :c2] = bf(-(gr[:c2] + u_ref[q]))
            wv_ref[q, c2:] = vs_ref[q]
            yr_ref[q] = gr[c2:]
        for p in pairs:
            q = ck * npair + p
            cs = slice(p * LANES, (p + 1) * LANES)
            wv = wv_ref[q]
            ys = yr_ref[q] + _mm(rbk_ref[q], wv)
            st_ref[p] = st_ref[p] * jnp.exp(cum_ref[(ck + 1) * cl - 1:(ck + 1) * cl, cs]) + _mm_tn(wv, end_ref[q])
            yc = jnp.where(own, ys - jnp.sum(ys, -1, keepdims=True) * inv_n, 0.0)
            yn = yc * lax.rsqrt(jnp.sum(yc * yc, -1, keepdims=True) * inv_n + RW_GN_EPS)
            out = (yn[:cl] + yn[cl:]) * gng_ref[:, cs] + gnb_ref[:, cs] + bon_ref[q]
            o_ref[rows, cs] = (out * gate_ref[rows, cs].astype(F32)).astype(BF16)


def _rwkv_layer(h, g_mix, mu, w_rkv, w0, w1, w2, a0, a1, a2, g1, g2, k_k, k_a, r_k, gn_g, gn_b, w_o, bsz, seq):
    m, d = h.shape
    row = lambda t: t.reshape(1, -1)
    tm = TOKEN_TILE
    proj_args = (h, h, g_mix, mu, w_rkv.astype(BF16), row(w0), w1.astype(BF16), w2.astype(BF16), row(a0),
                 a1.astype(BF16), a2.astype(BF16), g1.astype(BF16), g2.astype(BF16), row(k_k), row(k_a))
    before = pl.BlockSpec((SUBLANES, d), lambda i: (jnp.maximum(i * (tm // SUBLANES) - 1, 0), 0))
    dts = (BF16, F32, BF16, BF16, BF16, BF16, BF16)
    outs = pl.pallas_call(
        functools.partial(_rw_proj_body, seq=seq),
        out_shape=tuple(jax.ShapeDtypeStruct((m, d), dt) for dt in dts),
        grid=(m // tm,),
        in_specs=[_rows(tm, d), before] + [_full(a.shape) for a in proj_args[2:]],
        out_specs=tuple(_rows(tm, d) for _ in dts),
        compiler_params=_cparams("parallel"),
        name="rwkv_proj",
    )(*proj_args)
    cl = RW_CHUNK
    c2 = 2 * cl
    npair = d // LANES
    rows = RW_STEP_CHUNKS * cl
    ns = RW_STEP_CHUNKS * npair
    blk = pl.BlockSpec((None, rows, d), lambda b_, i: (b_, i, 0))
    vec = (row(r_k), row(gn_g), row(gn_b))
    o = pl.pallas_call(
        _rw_scan_body,
        out_shape=jax.ShapeDtypeStruct((bsz, seq, d), BF16),
        grid=(bsz, seq // rows),
        in_specs=[blk] * 7 + [pl.BlockSpec(a.shape, lambda b_, i: (0, 0)) for a in vec],
        out_specs=blk,
        scratch_shapes=[
            pltpu.VMEM((npair, c2, LANES), F32),
            pltpu.VMEM((rows, d), F32),
            pltpu.VMEM((ns, 2 * c2, LANES), BF16),
            pltpu.VMEM((ns, 2 * c2, LANES), BF16),
            pltpu.VMEM((ns, 2 * c2, LANES), BF16),
            pltpu.VMEM((ns, c2, LANES), BF16),
            pltpu.VMEM((ns, cl, LANES), F32),
            pltpu.VMEM((ns, c2, c2), BF16),
            pltpu.VMEM((ns, c2, c2), F32),
            pltpu.VMEM((ns, c2, c2), BF16),
            pltpu.VMEM((ns, c2, 2 * c2), BF16),
            pltpu.VMEM((ns, c2, LANES), BF16),
            pltpu.VMEM((ns, c2, LANES), BF16),
            pltpu.VMEM((ns, c2, LANES), F32),
            pltpu.VMEM((ns, 2 * c2, LANES), BF16),
            pltpu.VMEM((ns, c2, LANES), F32),
        ],
        compiler_params=_cparams("parallel", "arbitrary"),
        name="rwkv_scan",
    )(*[t.reshape(bsz, seq, d) for t in outs], *vec)
    return o.reshape(m, d), w_o.astype(BF16), jnp.zeros((1, d), F32)


def kernel(x, positions, norm_mix, norm_ffn, ffn_w_in, ffn_w_out, norm_final, rw_mu, rw_w_rkv, rw_w0, rw_w1, rw_w2, rw_a0, rw_a1, rw_a2, rw_g1, rw_g2, rw_k_k, rw_k_a, rw_r_k, rw_gn_g, rw_gn_b, rw_w_o, sw_w_qkv, sw_b_qkv, sw_sinks, sw_w_o, sw_b_o, sg_w_in, sg_b_in, sg_ln_g, sg_ln_b, sg_w_s, sg_b_s, sg_w_o, sg_b_o, gla_w_in, gla_w_a2, gla_b_a, gla_gn_g, gla_w_o):
    bsz, seq, d = x.shape
    depth = norm_mix.shape[0]
    h = x.reshape(bsz * seq, d)
    g_final = norm_final[None]
    for i in range(depth):
        t, j = i % 4, i // 4
        g_mix = norm_mix[i][None]
        pre = None
        if t == 0:
            pre = _rwkv_layer(h, g_mix, rw_mu[j], rw_w_rkv[j], rw_w0[j], rw_w1[j], rw_w2[j], rw_a0[j], rw_a1[j],
                            rw_a2[j], rw_g1[j], rw_g2[j], rw_k_k[j], rw_k_a[j], rw_r_k[j], rw_gn_g[j], rw_gn_b[j],
                            rw_w_o[j], bsz, seq)
        elif t == 1:
            pre = _swa_layer(h, positions, g_mix, sw_w_qkv[j], sw_b_qkv[j], sw_sinks[j], sw_w_o[j], sw_b_o[j], bsz, seq)
        elif t == 2:
            h = _sgu_layer(h, g_mix, sg_w_in[j], sg_b_in[j], sg_ln_g[j], sg_ln_b[j], sg_w_s[j], sg_b_s[j],
                           sg_w_o[j], sg_b_o[j])
        else:
            h = _gla_layer(h, g_mix, gla_w_in[j], gla_w_a2[j], gla_b_a[j], gla_gn_g[j], gla_w_o[j], bsz, seq)
        h = _ffn(h, norm_ffn[i][None], ffn_w_in, ffn_w_out, g_final, i, pre=pre, final=i == depth - 1)
    return h.reshape(bsz, seq, d)
```
